```python
import jax
import jax.numpy as jnp
from jax import lax
import numpy as np

D_MODEL = 1024
BATCH = 4
SEQ = 4096
DEPTH = 2

GRID_W = 64
CTX_LEN = 256

A_HEADS = 4
A_DK = 128
A_DV = 128
A_CHUNK = 64
LB_TINY = 1e-30
B_HEADS = 8
B_KV_HEADS = 2
B_GROUP = B_HEADS // B_KV_HEADS
B_HD = 64
B_QBLOCK = 128
ROPE_THETA = 10000.0
C_HEADS = 8
C_HD = 64
WIN_R = 8
WIN_C = 16

BRANCH_W = 512
N_BRANCH = 3
FFN_HIDDEN = ((8 * D_MODEL + 3 * 256 - 1) // (3 * 256)) * 256
EPS = 1e-6

SPLITS = (A_HEADS * A_DK, A_HEADS * A_DK, A_HEADS * A_DK, A_HEADS * A_DV, A_HEADS * A_DV,
          B_HEADS * B_HD, B_KV_HEADS * B_HD, B_KV_HEADS * B_HD,
          C_HEADS * C_HD, C_HEADS * C_HD, C_HEADS * C_HD,
          N_BRANCH * D_MODEL)
IN_WIDTH = sum(SPLITS)

kernel_name = 'hybrid_hgrn2_gqa_natten_dit_block'


def rmsnorm(x, w):
    xf = x.astype(jnp.float32)
    y = xf * lax.rsqrt(jnp.mean(xf * xf, axis=-1, keepdims=True) + EPS)
    return (y * w.astype(jnp.float32)).astype(x.dtype)


def project(u, w):
    return jnp.split(u @ w, np.cumsum(SPLITS)[:-1].tolist(), axis=-1)


def to_heads(a, h):
    bn, t, _ = a.shape
    return a.reshape(bn, t, h, -1).transpose(0, 2, 1, 3)


def merge_heads(o):
    bn, h, t, hd = o.shape
    return o.transpose(0, 2, 1, 3).reshape(bn, t, h * hd)


def merge_gqa_heads(o):
    bn, kv, g, t, hd = o.shape
    return o.transpose(0, 3, 1, 2, 4).reshape(bn, t, kv * g * hd)


def hgrn2_forget(z, lb):
    zf = z.astype(jnp.float32)
    lb = lb.astype(jnp.float32)
    log_f = jnp.logaddexp(jnp.log(lb + LB_TINY), jnp.log1p(-lb) + jax.nn.log_sigmoid(zf))
    k = (1.0 - lb) * jax.nn.sigmoid(-zf)
    return log_f, k


def hgrn2_prep(aq, af_f, af_b, ai, lb):
    q = to_heads(aq.astype(jnp.float32), A_HEADS) * (A_DK ** -0.5)
    lf_f, k_f = hgrn2_forget(af_f, lb[0])
    lf_b, k_b = hgrn2_forget(af_b, lb[1])
    v = to_heads(ai.astype(jnp.float32), A_HEADS)
    return (q, to_heads(lf_f, A_HEADS), to_heads(k_f, A_HEADS),
            to_heads(lf_b, A_HEADS), to_heads(k_b, A_HEADS), v)


def gla_chunked(q, k, v, log_f, s0):
    bn, h, t, dk = q.shape
    dv = v.shape[-1]
    n = t // A_CHUNK

    def chunks(a):
        return jnp.moveaxis(a.reshape(bn, h, n, A_CHUNK, a.shape[-1]), 2, 0)

    tri = jnp.tril(jnp.ones((A_CHUNK, A_CHUNK), dtype=bool))

    def step(s, inp):
        qc, kc, vc, gc = inp
        b = jnp.cumsum(gc, axis=2)
        bl = b[:, :, -1:, :]
        o_inter = jnp.einsum('bhtk,bhkv->bhtv', qc * jnp.exp(b), s)
        diff = b[:, :, :, None, :] - b[:, :, None, :, :]
        decay = jnp.where(tri[:, :, None], jnp.exp(jnp.minimum(diff, 0.0)), 0.0)
        scores = jnp.einsum('bhtsk,bhsk->bhts', qc[:, :, :, None, :] * decay, kc)
        o_intra = jnp.einsum('bhts,bhsv->bhtv', scores, vc)
        s_new = jnp.exp(bl[:, :, 0, :])[..., None] * s + jnp.einsum('bhsk,bhsv->bhkv', kc * jnp.exp(bl - b), vc)
        return s_new, o_inter + o_intra

    s_fin, o = lax.scan(step, s0, (chunks(q), chunks(k), chunks(v), chunks(log_f)))
    o = jnp.moveaxis(o, 0, 2).reshape(bn, h, t, dv)
    return o, s_fin


def hgrn2_bidir(q, lf_f, k_f, lf_b, k_b, v, s_f0, s_b0):
    o_f, s_f = gla_chunked(q, k_f, v, lf_f, s_f0)
    flip = lambda a: jnp.flip(a, axis=2)
    o_b, s_b = gla_chunked(flip(q), flip(k_b), flip(v), flip(lf_b), s_b0)
    return o_f + flip(o_b), s_f, s_b


def hgrn2_readout(o, og, gn):
    bn, h, t, dv = o.shape
    g = og.reshape(bn, t, h, dv).astype(jnp.float32)
    y = rmsnorm(o.transpose(0, 2, 1, 3), gn) * jax.nn.silu(g)
    return y.reshape(bn, t, h * dv).astype(og.dtype)


def axial_rope(t):
    pos = jnp.arange(t)
    row = (pos // GRID_W).astype(jnp.float32)
    col = (pos % GRID_W).astype(jnp.float32)
    n = B_HD // 4
    inv = ROPE_THETA ** (-jnp.arange(n, dtype=jnp.float32) / n)
    ang = jnp.concatenate([row[:, None] * inv, col[:, None] * inv], axis=-1)
    return jnp.cos(ang), jnp.sin(ang)


def apply_rope(x, cos, sin):
    bn, t, h, hd = x.shape
    xp = x.astype(jnp.float32).reshape(bn, t, h, hd // 2, 2)
    x1, x2 = xp[..., 0], xp[..., 1]
    cs = cos[None, :, None, :]
    sn = sin[None, :, None, :]
    out = jnp.stack([x1 * cs - x2 * sn, x1 * sn + x2 * cs], axis=-1).reshape(bn, t, h, hd)
    return out.astype(x.dtype)


def gqa_prep(bq, bk, bv, qn, kn, rope):
    bn, t, _ = bq.shape
    q = rmsnorm(bq.reshape(bn, t, B_HEADS, B_HD), qn)
    k = rmsnorm(bk.reshape(bn, t, B_KV_HEADS, B_HD), kn)
    if rope is not None:
        q = apply_rope(q, rope[0], rope[1])
        k = apply_rope(k, rope[0], rope[1])
    q = q.reshape(bn, t, B_KV_HEADS, B_GROUP, B_HD).transpose(0, 2, 3, 1, 4)
    k = k.transpose(0, 2, 1, 3)
    v = bv.reshape(bn, t, B_KV_HEADS, B_HD).transpose(0, 2, 1, 3)
    return q, k, v


def softmax_attend(q, k, v):
    s = jnp.einsum('bngqd,bnkd->bngqk', q, k).astype(jnp.float32) * (q.shape[-1] ** -0.5)
    p = jax.nn.softmax(s, axis=-1).astype(v.dtype)
    return jnp.einsum('bngqk,bnkd->bngqd', p, v)


def gqa_latent(q, k, v):
    bn, kv, g, t, hd = q.shape
    nb = t // B_QBLOCK
    qb = jnp.moveaxis(q.reshape(bn, kv, g, nb, B_QBLOCK, hd), 3, 0)
    ob = lax.map(lambda qi: softmax_attend(qi, k, v), qb)
    return jnp.moveaxis(ob, 0, 3).reshape(bn, kv, g, t, hd)


def nat_prep(cq, ck, cv, qn, kn):
    bn, t, _ = cq.shape
    q = rmsnorm(cq.reshape(bn, t, C_HEADS, C_HD), qn).transpose(0, 2, 1, 3)
    k = rmsnorm(ck.reshape(bn, t, C_HEADS, C_HD), kn).transpose(0, 2, 1, 3)
    v = cv.reshape(bn, t, C_HEADS, C_HD).transpose(0, 2, 1, 3)
    return q, k, v


def neighborhood_latent(q, k, v, kc, vc, bias_tab):
    bn, h, t, hd = q.shape
    rows = t // GRID_W
    kr = min(WIN_R, rows)
    nw = kr * WIN_C
    col = jnp.arange(GRID_W)
    c0 = jnp.clip(col - WIN_C // 2, 0, GRID_W - WIN_C)
    win_cols = c0[:, None] + jnp.arange(WIN_C)
    dc = win_cols - col[:, None] + (WIN_C - 1)
    bias_tab = bias_tab.astype(jnp.float32)
    scale = hd ** -0.5

    def row_block(r):
        r0 = jnp.clip(r - kr // 2, 0, rows - kr)
        win_rows = r0 + jnp.arange(kr)
        idx = (win_rows[None, :, None] * GRID_W + win_cols[:, None, :]).reshape(GRID_W, nw)
        dr = win_rows - r + (WIN_R - 1)
        bias = bias_tab[:, dr[None, :, None], dc[:, None, :]].reshape(h, GRID_W, nw)
        qr = lax.dynamic_slice_in_dim(q, r * GRID_W, GRID_W, axis=2)
        kg = k[:, :, idx, :]
        vg = v[:, :, idx, :]
        s_win = jnp.einsum('bhqd,bhqkd->bhqk', qr, kg).astype(jnp.float32) * scale + bias
        s_ctx = jnp.einsum('bhqd,bhcd->bhqc', qr, kc).astype(jnp.float32) * scale
        p = jax.nn.softmax(jnp.concatenate([s_win, s_ctx], axis=-1), axis=-1).astype(v.dtype)
        return (jnp.einsum('bhqk,bhqkd->bhqd', p[..., :nw], vg)
                + jnp.einsum('bhqc,bhcd->bhqd', p[..., nw:], vc))

    out = lax.map(row_block, jnp.arange(rows))
    return jnp.moveaxis(out, 0, 2).reshape(bn, h, t, hd)


def merge_branches(ya, yb, yc, gates, w_br, w_o):
    bn, t, _ = ya.shape
    y = jnp.stack([ya, yb, yc], axis=2)
    proj = jnp.einsum('btkw,kwd->btkd', y, w_br)
    g = jax.nn.sigmoid(gates.reshape(bn, t, N_BRANCH, -1))
    return jnp.sum(g * proj, axis=2) @ w_o


def swiglu(h, w_gu, w_d):
    a, b = jnp.split(h @ w_gu, 2, axis=-1)
    return (jax.nn.silu(a) * b) @ w_d


def setup_inputs(seed: int = 0) -> dict:
    key = jax.random.key(seed)
    ks = jax.random.split(key, 20)
    nrm = jax.random.normal
    d = D_MODEL
    f32 = jnp.float32
    return {
        'x': nrm(ks[0], (BATCH, SEQ, d), f32),
        'c': nrm(ks[1], (BATCH, d), f32),
        'ctx': nrm(ks[2], (BATCH, CTX_LEN, d), f32),
        'c_ctx': nrm(ks[3], (d,), f32),
        'w_mod': nrm(ks[4], (DEPTH, d, 6 * d), f32) * (0.5 * d ** -0.5),
        'b_mod': nrm(ks[5], (DEPTH, 6 * d), f32) * 0.02,
        'norm_mix': 1.0 + 0.1 * nrm(ks[6], (DEPTH, d), f32),
        'norm_ffn': 1.0 + 0.1 * nrm(ks[7], (DEPTH, d), f32),
        'w_in': nrm(ks[8], (DEPTH, d, IN_WIDTH), f32) * d ** -0.5,
        'lb_raw': nrm(ks[9], (DEPTH, 2, A_HEADS * A_DK), f32),
        'gn_a': 1.0 + 0.1 * nrm(ks[10], (DEPTH, A_DV), f32),
        'qn_b': 1.0 + 0.1 * nrm(ks[11], (DEPTH, B_HD), f32),
        'kn_b': 1.0 + 0.1 * nrm(ks[12], (DEPTH, B_HD), f32),
        'qn_c': 1.0 + 0.1 * nrm(ks[13], (DEPTH, C_HD), f32),
        'kn_c': 1.0 + 0.1 * nrm(ks[14], (DEPTH, C_HD), f32),
        'rel_bias': 0.5 * nrm(ks[15], (DEPTH, C_HEADS, 2 * WIN_R - 1, 2 * WIN_C - 1), f32),
        'w_branch': nrm(ks[16], (DEPTH, N_BRANCH, BRANCH_W, d), f32) * BRANCH_W ** -0.5,
        'w_out': nrm(ks[17], (DEPTH, d, d), f32) * d ** -0.5,
        'w_gate_up': nrm(ks[18], (DEPTH, d, 2 * FFN_HIDDEN), f32) * d ** -0.5,
        'w_down': nrm(ks[19], (DEPTH, FFN_HIDDEN, d), f32) * FFN_HIDDEN ** -0.5,
    }


def reference(x, c, ctx, c_ctx, w_mod, b_mod, norm_mix, norm_ffn, w_in, lb_raw, gn_a,
              qn_b, kn_b, qn_c, kn_c, rel_bias, w_branch, w_out, w_gate_up, w_down):
    bn, t, _ = x.shape
    rope = axial_rope(t)
    lbp = jax.nn.softmax(lb_raw.astype(jnp.float32), axis=0)
    lb_all = jnp.clip(jnp.cumsum(lbp, axis=0) - lbp[:1], 0.0, 1.0 - 1e-6)
    s_zero = jnp.zeros((bn, A_HEADS, A_DK, A_DV), jnp.float32)
    xc = ctx
    for l in range(DEPTH):
        ctx_next = l < DEPTH - 1
        mod = (jax.nn.silu(c) @ w_mod[l] + b_mod[l])[:, None, :]
        mod_c = jax.nn.silu(c_ctx) @ w_mod[l] + b_mod[l]
        sh1, sc1, g1, sh2, sc2, g2 = jnp.split(mod, 6, axis=-1)
        csh1, csc1, cg1, csh2, csc2, cg2 = jnp.split(mod_c, 6, axis=-1)
        u = rmsnorm(x, norm_mix[l]) * (1.0 + sc1) + sh1
        uc = rmsnorm(xc, norm_mix[l]) * (1.0 + csc1) + csh1
        aq, af_f, af_b, ai, ag, bq, bk, bv, cq, ck, cv, gt = project(u, w_in[l])
        caq, caf_f, caf_b, cai, cag, cbq, cbk, cbv, ccq, cck, ccv, cgt = project(uc, w_in[l])

        o_a_ctx, s_f, s_b = hgrn2_bidir(*hgrn2_prep(caq, caf_f, caf_b, cai, lb_all[l]), s_zero, s_zero)
        o_a, _, _ = hgrn2_bidir(*hgrn2_prep(aq, af_f, af_b, ai, lb_all[l]), s_f, s_b)
        ya = hgrn2_readout(o_a, ag, gn_a[l])

        qb, kb, vb = gqa_prep(bq, bk, bv, qn_b[l], kn_b[l], rope)
        qbc, kbc, vbc = gqa_prep(cbq, cbk, cbv, qn_b[l], kn_b[l], None)
        yb = merge_gqa_heads(gqa_latent(qb, jnp.concatenate([kb, kbc], axis=2),
                                        jnp.concatenate([vb, vbc], axis=2)))

        qn_, kn_, vn_ = nat_prep(cq, ck, cv, qn_c[l], kn_c[l])
        qnc, knc, vnc = nat_prep(ccq, cck, ccv, qn_c[l], kn_c[l])
        yc = merge_heads(neighborhood_latent(qn_, kn_, vn_, knc, vnc, rel_bias[l]))

        x = x + g1 * merge_branches(ya, yb, yc, gt, w_branch[l], w_out[l])
        h = rmsnorm(x, norm_ffn[l]) * (1.0 + sc2) + sh2
        x = x + g2 * swiglu(h, w_gate_up[l], w_down[l])

        if ctx_next:
            ya_c = hgrn2_readout(o_a_ctx, cag, gn_a[l])
            yb_c = merge_gqa_heads(softmax_attend(qbc, kbc, vbc))
            yc_c = merge_heads(softmax_attend(qnc[:, :, None], knc, vnc)[:, :, 0])
            xc = xc + cg1 * merge_branches(ya_c, yb_c, yc_c, cgt, w_branch[l], w_out[l])
            hc = rmsnorm(xc, norm_ffn[l]) * (1.0 + csc2) + csh2
            xc = xc + cg2 * swiglu(hc, w_gate_up[l], w_down[l])
    return x
```

```python
import functools

import numpy as np
import jax
import jax.numpy as jnp
from jax import lax
from jax.experimental import pallas as pl
from jax.experimental.pallas import tpu as pltpu

GRID_W = 64
A_HEADS = 4
A_DK = 128
B_HEADS = 8
B_KV_HEADS = 2
B_GROUP = B_HEADS // B_KV_HEADS
HEAD_DIM = 64
C_HEADS = 8
WIN_R = 8
WIN_C = 16
N_BRANCH = 3
ROPE_THETA = 10000.0
EPS = 1e-6
LB_TINY = 1e-30

LANES = 128
TOKEN_TILE = 256
A_CHUNK = 64
A_LEVELS = 6
KV_TILE = 256
Q_TILE = 128
NEG_BIG = -1e30
VMEM_LIMIT = 56 * 1024 * 1024

F32 = jnp.float32
BF16 = jnp.bfloat16


def _dot(a, b):
    return jnp.dot(a, b, preferred_element_type=F32)


def _dot_nt(a, b):
    return lax.dot_general(a, b, (((1,), (1,)), ((), ())), preferred_element_type=F32)


def _dot_tn(a, b):
    return lax.dot_general(a, b, (((0,), (0,)), ((), ())), preferred_element_type=F32)


def _split_bf16(x):
    hi = x.astype(BF16)
    lo = (x - hi.astype(F32)).astype(BF16)
    return hi, lo


def _silu(x):
    return x * jax.nn.sigmoid(x)


def _params(*sem):
    return pltpu.CompilerParams(dimension_semantics=sem, vmem_limit_bytes=VMEM_LIMIT)


def _resident(shape, index_map):
    return pl.BlockSpec(shape, index_map, pipeline_mode=pl.Buffered(1))


def _mod_kernel(c_ref, w_ref, b_ref, o_ref):
    a = _silu(c_ref[...])
    a_hi, a_lo = _split_bf16(a)
    w = w_ref[0]
    w_hi, w_lo = _split_bf16(w)
    acc = _dot(a_hi, w_hi) + _dot(a_hi, w_lo) + _dot(a_lo, w_hi)
    o_ref[0] = acc + b_ref[0]


def _modulation(c_all, w_mod, b_mod):
    depth, d, n = w_mod.shape
    rows = c_all.shape[0]
    tn = 1536
    return pl.pallas_call(
        _mod_kernel,
        grid=(depth, n // tn),
        in_specs=[
            pl.BlockSpec((rows, d), lambda l, j: (0, 0)),
            pl.BlockSpec((1, d, tn), lambda l, j: (l, 0, j)),
            pl.BlockSpec((1, 1, tn), lambda l, j: (l, 0, j)),
        ],
        out_specs=pl.BlockSpec((1, rows, tn), lambda l, j: (l, 0, j)),
        out_shape=jax.ShapeDtypeStruct((depth, rows, n), F32),
        compiler_params=_params("arbitrary", "arbitrary"),
        name="modulation",
    )(c_all, w_mod, b_mod.reshape(depth, 1, n))


def _group_mean_sq(x, gmat):
    hi, lo = _split_bf16(x * x)
    return _dot(hi, gmat) + _dot(lo, gmat)


def _head_norm(x, gmat, w):
    return x * lax.rsqrt(_group_mean_sq(x, gmat) + EPS) * w


def _rope(x, cos, sin_signed, first_half):
    rot = jnp.where(first_half, pltpu.roll(x, LANES - HEAD_DIM // 2, 1),
                    pltpu.roll(x, HEAD_DIM // 2, 1))
    return x * cos + rot * sin_signed


def _log_forget(z, la, l1):
    ls = jnp.minimum(z, 0.0) - jnp.log1p(jnp.exp(-jnp.abs(z)))
    t = l1 + ls
    return jnp.maximum(la, t) + jnp.log1p(jnp.exp(-jnp.abs(la - t)))


def _inproj_kernel(x_ref, mod_ref, nw_ref, w_ref, lbc_ref, hw_ref, cos_ref, sin_ref, gmat_ref,
                   aq_ref, alff_ref, akf_ref, alfb_ref, akb_ref, av_ref, aog_ref,
                   bq_ref, bk_ref, bv_ref, cq_ref, ck_ref, cv_ref, gt_ref, *, d):
    x = x_ref[0]
    ms = jnp.mean(x * x, axis=-1, keepdims=True)
    y = x * lax.rsqrt(ms + EPS) * nw_ref[...]
    sh1 = mod_ref[0, 0, :, 0:d]
    sc1 = mod_ref[0, 0, :, d:2 * d]
    u = (y * (1.0 + sc1) + sh1).astype(BF16)

    aw = A_HEADS * A_DK
    col = [0]

    def proj(width):
        c0 = col[0]
        col[0] = c0 + width
        return _dot(u, w_ref[:, c0:c0 + width])

    aq_ref[0] = (proj(aw) * (A_DK ** -0.5)).astype(BF16)
    for lf_ref, k_ref, row in ((alff_ref, akf_ref, 0), (alfb_ref, akb_ref, 1)):
        z = proj(aw)
        la = lbc_ref[row:row + 1, :]
        l1 = lbc_ref[2 + row:3 + row, :]
        oml = lbc_ref[4 + row:5 + row, :]
        lf_ref[0] = _log_forget(z, la, l1)
        k_ref[0] = (oml * jax.nn.sigmoid(-z)).astype(BF16)
    av_ref[0] = proj(aw).astype(BF16)
    aog_ref[0] = proj(aw).astype(BF16)

    gmat = gmat_ref[...]
    lane = lax.broadcasted_iota(jnp.int32, (x.shape[0], LANES), 1)
    first_half = (lane % HEAD_DIM) < (HEAD_DIM // 2)
    cos = cos_ref[...]
    sin = sin_ref[...]
    scale = HEAD_DIM ** -0.5

    zq = proj(B_HEADS * HEAD_DIM)
    for j in range(B_HEADS * HEAD_DIM // LANES):
        sl = slice(j * LANES, (j + 1) * LANES)
        qn = _head_norm(zq[:, sl], gmat, hw_ref[0:1, sl])
        bq_ref[0, :, sl] = (_rope(qn, cos, sin, first_half) * scale).astype(BF16)
    zk = proj(B_KV_HEADS * HEAD_DIM)
    kn = _head_norm(zk, gmat, hw_ref[1:2, 0:LANES])
    bk_ref[0] = _rope(kn, cos, sin, first_half).astype(BF16)
    bv_ref[0] = proj(B_KV_HEADS * HEAD_DIM).astype(BF16)

    zq = proj(C_HEADS * HEAD_DIM)
    for j in range(C_HEADS * HEAD_DIM // LANES):
        sl = slice(j * LANES, (j + 1) * LANES)
        cq_ref[0, :, sl] = (_head_norm(zq[:, sl], gmat, hw_ref[2:3, sl]) * scale).astype(BF16)
    zk = proj(C_HEADS * HEAD_DIM)
    for j in range(C_HEADS * HEAD_DIM // LANES):
        sl = slice(j * LANES, (j + 1) * LANES)
        ck_ref[0, :, sl] = _head_norm(zk[:, sl], gmat, hw_ref[3:4, sl]).astype(BF16)
    cv_ref[0] = proj(C_HEADS * HEAD_DIM).astype(BF16)

    for k in range(N_BRANCH):
        gt_ref[0, :, k * d:(k + 1) * d] = jax.nn.sigmoid(proj(d)).astype(BF16)


def _in_projection(xs, modsel, norm_w, w_bf, lbc, headw, cos_t, sin_t, gmat, n_ctx_tiles):
    b, s, d = xs.shape
    tm = TOKEN_TILE
    aw = A_HEADS * A_DK
    n_in = w_bf.shape[1]
    widths = [(aw, BF16), (aw, F32), (aw, BF16), (aw, F32), (aw, BF16), (aw, BF16), (aw, BF16),
              (B_HEADS * HEAD_DIM, BF16), (B_KV_HEADS * HEAD_DIM, BF16), (B_KV_HEADS * HEAD_DIM, BF16),
              (C_HEADS * HEAD_DIM, BF16), (C_HEADS * HEAD_DIM, BF16), (C_HEADS * HEAD_DIM, BF16),
              (N_BRANCH * d, BF16)]
    tok = lambda w: pl.BlockSpec((1, tm, w), lambda i, j: (i, j, 0))
    return pl.pallas_call(
        functools.partial(_inproj_kernel, d=d),
        grid=(b, s // tm),
        in_specs=[
            tok(d),
            pl.BlockSpec((1, 1, 1, modsel.shape[-1]),
                         lambda i, j: (i, (j >= n_ctx_tiles).astype(jnp.int32), 0, 0)),
            _resident((1, d), lambda i, j: (0, 0)),
            _resident((d, n_in), lambda i, j: (0, 0)),
            _resident(lbc.shape, lambda i, j: (0, 0)),
            _resident(headw.shape, lambda i, j: (0, 0)),
            pl.BlockSpec((tm, LANES), lambda i, j: (j, 0)),
            pl.BlockSpec((tm, LANES), lambda i, j: (j, 0)),
            _resident((LANES, LANES), lambda i, j: (0, 0)),
        ],
        out_specs=[tok(w) for w, _ in widths],
        out_shape=[jax.ShapeDtypeStruct((b, s, w), dt) for w, dt in widths],
        compiler_params=_params("parallel", "parallel"),
        name="in_projection",
    )(xs, modsel, norm_w, w_bf, lbc, headw, cos_t, sin_t, gmat)


def _hgrn_tables():
    L = A_CHUNK
    ms = [1 << i for i in range(A_LEVELS)]
    dmat = np.zeros((2, (2 + A_LEVELS) * L, L), np.float32)
    smask = np.zeros((2, A_LEVELS + 1, L, L), np.float32)
    qrole = np.zeros((2, A_LEVELS, L, LANES), np.float32)
    t = np.arange(L)
    for dr in range(2):
        fwd = dr == 0
        for r in range(L):
            if fwd:
                dmat[dr, r, :r + 1] = 1.0
                dmat[dr, L + r, r + 1:] = 1.0
            else:
                dmat[dr, r, r:] = 1.0
                dmat[dr, L + r, :r] = 1.0
        for li, m in enumerate(ms):
            base = (2 + li) * L
            for r in range(L):
                start = (r // (2 * m)) * 2 * m
                mid = start + m
                upper = r >= mid
                if fwd:
                    if upper:
                        dmat[dr, base + r, mid:r + 1] = 1.0
                    else:
                        dmat[dr, base + r, r + 1:mid] = 1.0
                else:
                    if not upper:
                        dmat[dr, base + r, r:mid] = 1.0
                    else:
                        dmat[dr, base + r, mid:r] = 1.0
                qrole[dr, li, r, :] = 1.0 if (upper == fwd) else 0.0
            same = (t[:, None] // (2 * m)) == (t[None, :] // (2 * m))
            t_up = (t[:, None] % (2 * m)) >= m
            s_up = (t[None, :] % (2 * m)) >= m
            if fwd:
                smask[dr, li] = same & t_up & ~s_up
            else:
                smask[dr, li] = same & ~t_up & s_up
        smask[dr, A_LEVELS] = np.eye(L)
    return dmat, smask, qrole


def _hgrn_kernel(q_ref, lff_ref, kf_ref, lfb_ref, kb_ref, v_ref, dmat_ref, smask_ref, qrole_ref,
                 o_ref, stf_ref, stb_ref, *, n_ctx_chunks, n_chunks):
    L = A_CHUNK
    stf_ref[...] = jnp.zeros_like(stf_ref)
    stb_ref[...] = jnp.zeros_like(stb_ref)
    o_ref[...] = jnp.zeros_like(o_ref)

    def chunk(c, dr, lf_ref, k_ref, st_ref):
        r0 = pl.multiple_of(c * L, L)
        rows = pl.ds(r0, L)
        g = lf_ref[0, rows, :]
        q = q_ref[0, rows, :].astype(F32)
        k = k_ref[0, rows, :].astype(F32)
        vb = v_ref[0, rows, :]
        g_hi, g_lo = _split_bf16(g)
        dm = dmat_ref[dr]
        ex = jnp.exp(_dot(dm, g_hi) + _dot(dm, g_lo))
        eb = ex[0:L]
        ekd = ex[L:2 * L]
        ebl = eb[L - 1:L] if dr == 0 else eb[0:1]
        st = st_ref[...]
        o = _dot_nt((q * eb).astype(BF16), st.astype(BF16))
        st_ref[...] = st * ebl + _dot_tn(vb, (k * ekd).astype(BF16))
        scores = jnp.sum(q * k, axis=-1, keepdims=True) * smask_ref[dr, A_LEVELS]
        for li in range(A_LEVELS):
            role = qrole_ref[dr, li] > 0.5
            xl = (jnp.where(role, q, k) * ex[(2 + li) * L:(3 + li) * L]).astype(BF16)
            scores = scores + _dot_nt(xl, xl) * smask_ref[dr, li]
        o = o + _dot(scores.astype(BF16), vb)
        o_ref[0, rows, :] = o_ref[0, rows, :] + o

    def body(i, carry):
        chunk(i, 0, lff_ref, kf_ref, stf_ref)
        cb = jnp.where(i < n_ctx_chunks, n_ctx_chunks - 1 - i, n_chunks + n_ctx_chunks - 1 - i)
        chunk(cb, 1, lfb_ref, kb_ref, stb_ref)
        return carry

    lax.fori_loop(0, n_chunks, body, 0)


def _hgrn(aq, alff, akf, alfb, akb, av, tables, n_ctx):
    b, s, _ = aq.shape
    dmat, smask, qrole = tables
    blk = lambda: pl.BlockSpec((1, s, A_DK), lambda i, h: (i, 0, h))
    const = lambda a: _resident(a.shape, lambda i, h: (0,) * a.ndim)
    return pl.pallas_call(
        functools.partial(_hgrn_kernel, n_ctx_chunks=n_ctx // A_CHUNK, n_chunks=s // A_CHUNK),
        grid=(b, A_HEADS),
        in_specs=[blk(), blk(), blk(), blk(), blk(), blk(), const(dmat), const(smask), const(qrole)],
        out_specs=blk(),
        out_shape=jax.ShapeDtypeStruct((b, s, A_HEADS * A_DK), F32),
        scratch_shapes=[pltpu.VMEM((A_DK, A_DK), F32), pltpu.VMEM((A_DK, A_DK), F32)],
        compiler_params=_params("parallel", "parallel"),
        name="hgrn2_scan",
    )(aq, alff, akf, alfb, akb, av, dmat, smask, qrole)


def _gqa_kernel(q_ref, k_ref, v_ref, o_ref, qs_ref, m_ref, l_ref, acc_ref, *, n_ctx, s):
    tq = Q_TILE
    tk = KV_TILE
    j = pl.program_id(1)
    n_kt = jnp.where(j < n_ctx // tq, n_ctx // tk, s // tk)
    lane = lax.broadcasted_iota(jnp.int32, (tq, LANES), 1)
    upper = lane >= HEAD_DIM
    pair_out = [None, None]
    for kv in range(B_KV_HEADS):
        for gi in range(B_GROUP):
            h = kv * B_GROUP + gi
            blk = q_ref[0, :, (h // 2) * LANES:(h // 2 + 1) * LANES].astype(F32)
            if h % 2 != kv:
                blk = pltpu.roll(blk, HEAD_DIM, 1)
            keep = upper if kv == 1 else jnp.logical_not(upper)
            qs_ref[gi * tq:(gi + 1) * tq, :] = jnp.where(keep, blk, 0.0).astype(BF16)
        m_ref[...] = jnp.full_like(m_ref, NEG_BIG)
        l_ref[...] = jnp.zeros_like(l_ref)
        acc_ref[...] = jnp.zeros_like(acc_ref)

        def body(c, carry):
            rows = pl.ds(pl.multiple_of(c * tk, tk), tk)
            sc = _dot_nt(qs_ref[...], k_ref[0, rows, :])
            m_prev = m_ref[...]
            m_new = jnp.maximum(m_prev, jnp.max(sc, axis=-1, keepdims=True))
            alpha = jnp.exp(m_prev - m_new)
            p = jnp.exp(sc - m_new)
            l_ref[...] = alpha * l_ref[...] + jnp.sum(p, axis=-1, keepdims=True)
            acc_ref[...] = alpha * acc_ref[...] + _dot(p.astype(BF16), v_ref[0, rows, :])
            m_ref[...] = m_new
            return carry

        lax.fori_loop(0, n_kt, body, 0)
        out = acc_ref[...] / l_ref[...]
        for gi in range(B_GROUP):
            h = kv * B_GROUP + gi
            oh = out[gi * tq:(gi + 1) * tq, :]
            if h % 2 != kv:
                oh = pltpu.roll(oh, HEAD_DIM, 1)
            pair_out[h % 2] = oh
            if h % 2 == 1:
                o_ref[0, :, (h // 2) * LANES:(h // 2 + 1) * LANES] = jnp.where(
                    upper, pair_out[1], pair_out[0]).astype(BF16)


def _gqa(bq, bk, bv, n_ctx):
    b, s, w = bq.shape
    tq = Q_TILE
    return pl.pallas_call(
        functools.partial(_gqa_kernel, n_ctx=n_ctx, s=s),
        grid=(b, s // tq),
        in_specs=[
            pl.BlockSpec((1, tq, w), lambda i, j: (i, j, 0)),
            pl.BlockSpec((1, s, LANES), lambda i, j: (i, 0, 0)),
            pl.BlockSpec((1, s, LANES), lambda i, j: (i, 0, 0)),
        ],
        out_specs=pl.BlockSpec((1, tq, w), lambda i, j: (i, j, 0)),
        out_shape=jax.ShapeDtypeStruct((b, s, w), BF16),
        scratch_shapes=[
            pltpu.VMEM((B_GROUP * tq, LANES), BF16),
            pltpu.VMEM((B_GROUP * tq, 1), F32),
            pltpu.VMEM((B_GROUP * tq, 1), F32),
            pltpu.VMEM((B_GROUP * tq, LANES), F32),
        ],
        compiler_params=_params("parallel", "parallel"),
        name="gqa_attention",
    )(bq, bk, bv)


def _natten_bias(rel_bias):
    qcol = np.arange(GRID_W)
    c0 = np.clip(qcol - WIN_C // 2, 0, GRID_W - WIN_C)
    kcol = np.arange(GRID_W)
    valid = (kcol[None, :] >= c0[:, None]) & (kcol[None, :] < c0[:, None] + WIN_C)
    dc = np.clip(kcol[None, :] - qcol[:, None] + (WIN_C - 1), 0, 2 * WIN_C - 2)
    cls = np.arange(WIN_R)
    jrow = np.arange(WIN_R)
    dr = jrow[None, :] - cls[:, None] + (WIN_R - 1)
    tab = rel_bias.astype(F32)[:, dr[:, None, :, None], dc[None, :, None, :]]
    tab = jnp.where(valid[None, None, :, None, :], tab, NEG_BIG)
    return tab.reshape(rel_bias.shape[0], WIN_R, GRID_W, WIN_R * GRID_W)


def _natten_kernel(q_ref, k_ref, v_ref, bias_ref, o_ref, *, n_ctx, grid_rows):
    w = GRID_W
    band = WIN_R * w
    lane = lax.broadcasted_iota(jnp.int32, (w, LANES), 1)
    upper = lane >= HEAD_DIM
    kc = k_ref[0, 0:n_ctx, :]
    vc = v_ref[0, 0:n_ctx, :]

    def heads(q, fn):
        outs = []
        for hh in range(2):
            keep = upper if hh == 1 else jnp.logical_not(upper)
            outs.append(fn(hh, jnp.where(keep, q, jnp.zeros_like(q))))
        return jnp.where(upper, outs[1], outs[0])

    def ctx_block(i, carry):
        rows = pl.ds(pl.multiple_of(i * w, w), w)

        def attend(hh, qh):
            sc = _dot_nt(qh, kc)
            p = jnp.exp(sc - jnp.max(sc, axis=-1, keepdims=True))
            return _dot(p.astype(BF16), vc) / jnp.sum(p, axis=-1, keepdims=True)

        o_ref[0, rows, :] = heads(q_ref[0, rows, :], attend).astype(BF16)
        return carry

    lax.fori_loop(0, n_ctx // w, ctx_block, 0)

    def row_block(r, carry):
        r0 = jnp.clip(r - WIN_R // 2, 0, grid_rows - WIN_R)
        cls = r - r0
        rows = pl.ds(pl.multiple_of(n_ctx + r * w, w), w)
        krows = pl.ds(pl.multiple_of(n_ctx + r0 * w, w), band)
        kb = k_ref[0, krows, :]
        vb = v_ref[0, krows, :]

        def attend(hh, qh):
            sw = _dot_nt(qh, kb) + bias_ref[hh, cls]
            sc = _dot_nt(qh, kc)
            m = jnp.maximum(jnp.max(sw, axis=-1, keepdims=True), jnp.max(sc, axis=-1, keepdims=True))
            pw = jnp.exp(sw - m)
            pc = jnp.exp(sc - m)
            den = jnp.sum(pw, axis=-1, keepdims=True) + jnp.sum(pc, axis=-1, keepdims=True)
            return (_dot(pw.astype(BF16), vb) + _dot(pc.astype(BF16), vc)) / den

        o_ref[0, rows, :] = heads(q_ref[0, rows, :], attend).astype(BF16)
        return carry

    lax.fori_loop(0, grid_rows, row_block, 0)


def _natten(cq, ck, cv, bias, n_ctx):
    b, s, w = cq.shape
    grid_rows = (s - n_ctx) // GRID_W
    assert grid_rows >= WIN_R
    blk = lambda: pl.BlockSpec((1, s, LANES), lambda i, p: (i, 0, p))
    return pl.pallas_call(
        functools.partial(_natten_kernel, n_ctx=n_ctx, grid_rows=grid_rows),
        grid=(b, w // LANES),
        in_specs=[blk(), blk(), blk(),
                  pl.BlockSpec((2, WIN_R, GRID_W, WIN_R * GRID_W), lambda i, p: (p, 0, 0, 0))],
        out_specs=blk(),
        out_shape=jax.ShapeDtypeStruct((b, s, w), BF16),
        compiler_params=_params("parallel", "parallel"),
        name="neighborhood_attention",
    )(cq, ck, cv, bias)


def _merge_kernel(x_ref, oa_ref, og_ref, yb_ref, yc_ref, gt_ref, mod_ref, gn_ref, wbr_ref, wo_ref,
                  o_ref, *, d):
    ya = []
    for h in range(A_HEADS):
        sl = slice(h * A_DK, (h + 1) * A_DK)
        oh = oa_ref[0, :, sl]
        yh = oh * lax.rsqrt(jnp.mean(oh * oh, axis=-1, keepdims=True) + EPS) * gn_ref[...]
        ya.append((yh * _silu(og_ref[0, :, sl].astype(F32))).astype(BF16))
    ys = (jnp.concatenate(ya, axis=-1), yb_ref[0], yc_ref[0])
    merged = None
    for k in range(N_BRANCH):
        term = gt_ref[0, :, k * d:(k + 1) * d].astype(F32) * _dot(ys[k], wbr_ref[k])
        merged = term if merged is None else merged + term
    g1 = mod_ref[0, 0, :, 2 * d:3 * d]
    o_ref[0] = x_ref[0] + g1 * _dot(merged.astype(BF16), wo_ref[...])


def _merge(xs, oa, og, yb, yc, gt, modsel, gn, wbr, wo, n_ctx_tiles, tile0, n_tiles):
    b, _, d = xs.shape
    tm = TOKEN_TILE
    tok = lambda w: pl.BlockSpec((1, tm, w), lambda i, j: (i, j + tile0, 0))
    return pl.pallas_call(
        functools.partial(_merge_kernel, d=d),
        grid=(b, n_tiles),
        in_specs=[
            tok(d), tok(oa.shape[-1]), tok(og.shape[-1]), tok(yb.shape[-1]), tok(yc.shape[-1]),
            tok(gt.shape[-1]),
            pl.BlockSpec((1, 1, 1, modsel.shape[-1]),
                         lambda i, j: (i, (j + tile0 >= n_ctx_tiles).astype(jnp.int32), 0, 0)),
            _resident((1, A_DK), lambda i, j: (0, 0)),
            _resident(wbr.shape, lambda i, j: (0, 0, 0)),
            _resident(wo.shape, lambda i, j: (0, 0)),
        ],
        out_specs=pl.BlockSpec((1, tm, d), lambda i, j: (i, j, 0)),
        out_shape=jax.ShapeDtypeStruct((b, n_tiles * tm, d), F32),
        compiler_params=_params("parallel", "parallel"),
        name="merge_out_projection",
    )(xs, oa, og, yb, yc, gt, modsel, gn, wbr, wo)


def _ffn_kernel(x_ref, mod_ref, nw_ref, wgu_ref, wd_ref, o_ref, *, d, hidden, n_split):
    x = x_ref[0]
    y = x * lax.rsqrt(jnp.mean(x * x, axis=-1, keepdims=True) + EPS) * nw_ref[...]
    sh2 = mod_ref[0, 0, :, 3 * d:4 * d]
    sc2 = mod_ref[0, 0, :, 4 * d:5 * d]
    g2 = mod_ref[0, 0, :, 5 * d:6 * d]
    h = (y * (1.0 + sc2) + sh2).astype(BF16)
    step = hidden // n_split
    acc = None
    for c in range(n_split):
        a = _dot(h, wgu_ref[:, c * step:(c + 1) * step])
        g = _dot(h, wgu_ref[:, hidden + c * step:hidden + (c + 1) * step])
        part = _dot((_silu(a) * g).astype(BF16), wd_ref[c * step:(c + 1) * step, :])
        acc = part if acc is None else acc + part
    o_ref[0] = x + g2 * acc


def _ffn(xs, modsel, norm_w, wgu, wd, n_ctx_tiles, tile0):
    b, s, d = xs.shape
    tm = TOKEN_TILE
    hidden = wd.shape[0]
    n_split = 2
    assert hidden % (n_split * LANES) == 0
    return pl.pallas_call(
        functools.partial(_ffn_kernel, d=d, hidden=hidden, n_split=n_split),
        grid=(b, s // tm),
        in_specs=[
            pl.BlockSpec((1, tm, d), lambda i, j: (i, j, 0)),
            pl.BlockSpec((1, 1, 1, modsel.shape[-1]),
                         lambda i, j: (i, (j + tile0 >= n_ctx_tiles).astype(jnp.int32), 0, 0)),
            _resident((1, d), lambda i, j: (0, 0)),
            _resident(wgu.shape, lambda i, j: (0, 0)),
            _resident(wd.shape, lambda i, j: (0, 0)),
        ],
        out_specs=pl.BlockSpec((1, tm, d), lambda i, j: (i, j, 0)),
        out_shape=jax.ShapeDtypeStruct((b, s, d), F32),
        compiler_params=_params("parallel", "parallel"),
        name="swiglu_ffn",
    )(xs, modsel, norm_w, wgu, wd)


def _rope_tables(t, n_ctx):
    pos = jnp.arange(t)
    row = (pos // GRID_W).astype(F32)
    colp = (pos % GRID_W).astype(F32)
    n = HEAD_DIM // 4
    inv = ROPE_THETA ** (-jnp.arange(n, dtype=F32) / n)
    ang = jnp.concatenate([row[:, None] * inv, colp[:, None] * inv], axis=-1)
    cos = jnp.cos(ang)
    sin = jnp.sin(ang)
    reps = LANES // HEAD_DIM
    cos_t = jnp.tile(jnp.concatenate([cos, cos], axis=-1), (1, reps))
    sin_t = jnp.tile(jnp.concatenate([-sin, sin], axis=-1), (1, reps))
    cos_t = jnp.concatenate([jnp.ones((n_ctx, LANES), F32), cos_t], axis=0)
    sin_t = jnp.concatenate([jnp.zeros((n_ctx, LANES), F32), sin_t], axis=0)
    return cos_t, sin_t


def _inproj_column_order(d):
    aw = A_HEADS * A_DK
    half = np.concatenate([np.arange(0, HEAD_DIM, 2), np.arange(1, HEAD_DIM, 2)])
    order = np.arange(5 * aw + (B_HEADS + 2 * B_KV_HEADS) * HEAD_DIM + 3 * C_HEADS * HEAD_DIM
                      + N_BRANCH * d)
    b0 = 5 * aw
    for h in range(B_HEADS + B_KV_HEADS):
        order[b0 + h * HEAD_DIM:b0 + (h + 1) * HEAD_DIM] = b0 + h * HEAD_DIM + half
    return order, half


def kernel(x, c, ctx, c_ctx, w_mod, b_mod, norm_mix, norm_ffn, w_in, lb_raw, gn_a, qn_b, kn_b,
           qn_c, kn_c, rel_bias, w_branch, w_out, w_gate_up, w_down):
    bn, t, d = x.shape
    n_ctx = ctx.shape[1]
    depth = w_mod.shape[0]
    tm = TOKEN_TILE
    assert n_ctx % tm == 0 and t % tm == 0 and t % GRID_W == 0
    n_ctx_tiles = n_ctx // tm
    aw = A_HEADS * A_DK

    c_all = jnp.zeros((8, d), F32).at[:bn].set(c).at[bn].set(c_ctx)
    mod = _modulation(c_all, w_mod, b_mod)
    modsel = jnp.stack([jnp.broadcast_to(mod[:, bn:bn + 1], (depth, bn, 6 * d)), mod[:, :bn]],
                       axis=2)[:, :, :, None, :]

    lbp = jax.nn.softmax(lb_raw.astype(F32), axis=0)
    lb_all = jnp.clip(jnp.cumsum(lbp, axis=0) - lbp[:1], 0.0, 1.0 - 1e-6)
    zeros2 = jnp.zeros_like(lb_all)
    lbc_all = jnp.concatenate([jnp.log(lb_all + LB_TINY), jnp.log1p(-lb_all), 1.0 - lb_all, zeros2],
                              axis=1)

    order, half = _inproj_column_order(d)
    cos_t, sin_t = _rope_tables(t, n_ctx)
    lane = np.arange(LANES)
    gmat = jnp.asarray((lane[:, None] // HEAD_DIM == lane[None, :] // HEAD_DIM) / HEAD_DIM, BF16)
    dmat, smask, qrole = _hgrn_tables()
    tables = (jnp.asarray(dmat, BF16), jnp.asarray(smask), jnp.asarray(qrole))

    xs = jnp.concatenate([ctx, x], axis=1)
    for l in range(depth):
        last = l == depth - 1
        w_bf = jnp.take(w_in[l], order, axis=1).astype(BF16)
        headw = jnp.stack([jnp.tile(qn_b[l][half], B_HEADS), jnp.tile(kn_b[l][half], B_HEADS),
                           jnp.tile(qn_c[l], C_HEADS), jnp.tile(kn_c[l], C_HEADS)]).astype(F32)
        headw = jnp.concatenate([headw, jnp.zeros((4, headw.shape[1]), F32)], axis=0)
        (aq, alff, akf, alfb, akb, av, aog, bq, bk, bv, cq, ck, cv, gt) = _in_projection(
            xs, modsel[l], norm_mix[l][None, :], w_bf, lbc_all[l], headw, cos_t, sin_t, gmat,
            n_ctx_tiles)
        oa = _hgrn(aq, alff, akf, alfb, akb, av, tables, n_ctx)
        yb = _gqa(bq, bk, bv, n_ctx)
        yc = _natten(cq, ck, cv, _natten_bias(rel_bias[l]), n_ctx)
        tile0 = n_ctx_tiles if last else 0
        n_tiles = (t if last else t + n_ctx) // tm
        xm = _merge(xs, oa, aog, yb, yc, gt, modsel[l], gn_a[l][None, :],
                    w_branch[l].astype(BF16), w_out[l].astype(BF16), n_ctx_tiles, tile0, n_tiles)
        xs = _ffn(xm, modsel[l], norm_ffn[l][None, :], w_gate_up[l].astype(BF16),
                  w_down[l].astype(BF16), n_ctx_tiles, tile0)
    return xs
```

```python
import functools

import numpy as np
import jax
import jax.numpy as jnp
from jax import lax
from jax.experimental import pallas as pl
from jax.experimental.pallas import tpu as pltpu

GRID_W = 64
A_HEADS = 4
A_DK = 128
B_HEADS = 8
B_KV_HEADS = 2
B_GROUP = B_HEADS // B_KV_HEADS
HEAD_DIM = 64
C_HEADS = 8
WIN_R = 8
WIN_C = 16
N_BRANCH = 3
ROPE_THETA = 10000.0
EPS = 1e-6
LB_TINY = 1e-30

LANES = 128
TOKEN_TILE = 256
A_CHUNK = 64
A_LEVELS = 6
KV_TILE = 512
Q_TILE = 128
NAT_ROWS = 4
NEG_BIG = -1e30
LOG2_E = 1.4426950408889634
BOUND_SLACK = 1.02
MAX_SAFE_SHIFT = 50.0
VMEM_LIMIT = 56 * 1024 * 1024

F32 = jnp.float32
BF16 = jnp.bfloat16


def _dot(a, b):
    return jnp.dot(a, b, preferred_element_type=F32)


def _dot_nt(a, b):
    return lax.dot_general(a, b, (((1,), (1,)), ((), ())), preferred_element_type=F32)


def _dot_tn(a, b):
    return lax.dot_general(a, b, (((0,), (0,)), ((), ())), preferred_element_type=F32)


def _split_bf16(x):
    hi = x.astype(BF16)
    lo = (x - hi.astype(F32)).astype(BF16)
    return hi, lo


def _silu(x):
    return x * jax.nn.sigmoid(x)


def _params(*sem):
    return pltpu.CompilerParams(dimension_semantics=sem, vmem_limit_bytes=VMEM_LIMIT)


def _resident(shape, index_map):
    return pl.BlockSpec(shape, index_map, pipeline_mode=pl.Buffered(1))


def _mod_kernel(c_ref, w_ref, b_ref, o_ref):
    a = _silu(c_ref[...])
    a_hi, a_lo = _split_bf16(a)
    w = w_ref[0]
    w_hi, w_lo = _split_bf16(w)
    acc = _dot(a_hi, w_hi) + _dot(a_hi, w_lo) + _dot(a_lo, w_hi)
    o_ref[0] = acc + b_ref[0]


def _modulation(c_all, w_mod, b_mod):
    depth, d, n = w_mod.shape
    rows = c_all.shape[0]
    tn = 1536
    return pl.pallas_call(
        _mod_kernel,
        grid=(depth, n // tn),
        in_specs=[
            pl.BlockSpec((rows, d), lambda l, j: (0, 0)),
            pl.BlockSpec((1, d, tn), lambda l, j: (l, 0, j)),
            pl.BlockSpec((1, 1, tn), lambda l, j: (l, 0, j)),
        ],
        out_specs=pl.BlockSpec((1, rows, tn), lambda l, j: (l, 0, j)),
        out_shape=jax.ShapeDtypeStruct((depth, rows, n), F32),
        compiler_params=_params("arbitrary", "arbitrary"),
        name="modulation",
    )(c_all, w_mod, b_mod.reshape(depth, 1, n))


def _group_mean_sq(x, gmat):
    hi, lo = _split_bf16(x * x)
    return _dot(hi, gmat) + _dot(lo, gmat)


def _head_norm(x, gmat, w):
    return x * lax.rsqrt(_group_mean_sq(x, gmat) + EPS) * w


def _rope(x, cos, sin_signed, first_half):
    rot = jnp.where(first_half, pltpu.roll(x, LANES - HEAD_DIM // 2, 1),
                    pltpu.roll(x, HEAD_DIM // 2, 1))
    return x * cos + rot * sin_signed


def _log_forget(z, la, l1):
    ls = jnp.minimum(z, 0.0) - jnp.log1p(jnp.exp(-jnp.abs(z)))
    t = l1 + ls
    return jnp.maximum(la, t) + jnp.log1p(jnp.exp(-jnp.abs(la - t)))


def _inproj_kernel(x_ref, mod_ref, nw_ref, w_ref, lbc_ref, hw_ref, cos_ref, sin_ref, gmat_ref,
                   aq_ref, alff_ref, akf_ref, alfb_ref, akb_ref, av_ref, aog_ref,
                   bq_ref, bk_ref, bv_ref, cq_ref, ck_ref, cv_ref, gt_ref, *, d):
    x = x_ref[0]
    ms = jnp.mean(x * x, axis=-1, keepdims=True)
    y = x * lax.rsqrt(ms + EPS) * nw_ref[...]
    sh1 = mod_ref[0, 0, :, 0:d]
    sc1 = mod_ref[0, 0, :, d:2 * d]
    u = (y * (1.0 + sc1) + sh1).astype(BF16)

    aw = A_HEADS * A_DK
    col = [0]

    def proj(width):
        c0 = col[0]
        col[0] = c0 + width
        return _dot(u, w_ref[:, c0:c0 + width])

    aq_ref[0] = (proj(aw) * (A_DK ** -0.5)).astype(BF16)
    for lf_ref, k_ref, row in ((alff_ref, akf_ref, 0), (alfb_ref, akb_ref, 1)):
        z = proj(aw)
        la = lbc_ref[row:row + 1, :]
        l1 = lbc_ref[2 + row:3 + row, :]
        oml = lbc_ref[4 + row:5 + row, :]
        lf_ref[0] = _log_forget(z, la, l1)
        k_ref[0] = (oml * jax.nn.sigmoid(-z)).astype(BF16)
    av_ref[0] = proj(aw).astype(BF16)
    aog_ref[0] = proj(aw).astype(BF16)

    gmat = gmat_ref[...]
    lane = lax.broadcasted_iota(jnp.int32, (x.shape[0], LANES), 1)
    first_half = (lane % HEAD_DIM) < (HEAD_DIM // 2)
    cos = cos_ref[...]
    sin = sin_ref[...]
    scale = HEAD_DIM ** -0.5

    zq = proj(B_HEADS * HEAD_DIM)
    for j in range(B_HEADS * HEAD_DIM // LANES):
        sl = slice(j * LANES, (j + 1) * LANES)
        qn = _head_norm(zq[:, sl], gmat, hw_ref[0:1, sl])
        bq_ref[0, :, sl] = (_rope(qn, cos, sin, first_half) * (scale * LOG2_E)).astype(BF16)
    zk = proj(B_KV_HEADS * HEAD_DIM)
    kn = _head_norm(zk, gmat, hw_ref[1:2, 0:LANES])
    bk_ref[0] = _rope(kn, cos, sin, first_half).astype(BF16)
    bv_ref[0] = proj(B_KV_HEADS * HEAD_DIM).astype(BF16)

    zq = proj(C_HEADS * HEAD_DIM)
    for j in range(C_HEADS * HEAD_DIM // LANES):
        sl = slice(j * LANES, (j + 1) * LANES)
        cq_ref[0, :, sl] = (_head_norm(zq[:, sl], gmat, hw_ref[2:3, sl]) * scale).astype(BF16)
    zk = proj(C_HEADS * HEAD_DIM)
    for j in range(C_HEADS * HEAD_DIM // LANES):
        sl = slice(j * LANES, (j + 1) * LANES)
        ck_ref[0, :, sl] = _head_norm(zk[:, sl], gmat, hw_ref[3:4, sl]).astype(BF16)
    cv_ref[0] = proj(C_HEADS * HEAD_DIM).astype(BF16)

    for k in range(N_BRANCH):
        gt_ref[0, :, k * d:(k + 1) * d] = jax.nn.sigmoid(proj(d)).astype(BF16)


def _in_projection(xs, modsel, norm_w, w_bf, lbc, headw, cos_t, sin_t, gmat, n_ctx_tiles):
    b, s, d = xs.shape
    tm = TOKEN_TILE
    aw = A_HEADS * A_DK
    n_in = w_bf.shape[1]
    widths = [(aw, BF16), (aw, F32), (aw, BF16), (aw, F32), (aw, BF16), (aw, BF16), (aw, BF16),
              (B_HEADS * HEAD_DIM, BF16), (B_KV_HEADS * HEAD_DIM, BF16), (B_KV_HEADS * HEAD_DIM, BF16),
              (C_HEADS * HEAD_DIM, BF16), (C_HEADS * HEAD_DIM, BF16), (C_HEADS * HEAD_DIM, BF16),
              (N_BRANCH * d, BF16)]
    tok = lambda w: pl.BlockSpec((1, tm, w), lambda i, j: (i, j, 0))
    return pl.pallas_call(
        functools.partial(_inproj_kernel, d=d),
        grid=(b, s // tm),
        in_specs=[
            tok(d),
            pl.BlockSpec((1, 1, 1, modsel.shape[-1]),
                         lambda i, j: (i, (j >= n_ctx_tiles).astype(jnp.int32), 0, 0)),
            _resident((1, d), lambda i, j: (0, 0)),
            _resident((d, n_in), lambda i, j: (0, 0)),
            _resident(lbc.shape, lambda i, j: (0, 0)),
            _resident(headw.shape, lambda i, j: (0, 0)),
            pl.BlockSpec((tm, LANES), lambda i, j: (j, 0)),
            pl.BlockSpec((tm, LANES), lambda i, j: (j, 0)),
            _resident((LANES, LANES), lambda i, j: (0, 0)),
        ],
        out_specs=[tok(w) for w, _ in widths],
        out_shape=[jax.ShapeDtypeStruct((b, s, w), dt) for w, dt in widths],
        compiler_params=_params("parallel", "parallel"),
        name="in_projection",
    )(xs, modsel, norm_w, w_bf, lbc, headw, cos_t, sin_t, gmat)


def _hgrn_tables():
    L = A_CHUNK
    ms = [1 << i for i in range(A_LEVELS)]
    dmat = np.zeros((2, (2 + A_LEVELS) * L, L), np.float32)
    smask = np.zeros((2, A_LEVELS + 1, L, L), np.float32)
    qrole = np.zeros((2, A_LEVELS, L, LANES), np.float32)
    t = np.arange(L)
    for dr in range(2):
        fwd = dr == 0
        for r in range(L):
            if fwd:
                dmat[dr, r, :r + 1] = 1.0
                dmat[dr, L + r, r + 1:] = 1.0
            else:
                dmat[dr, r, r:] = 1.0
                dmat[dr, L + r, :r] = 1.0
        for li, m in enumerate(ms):
            base = (2 + li) * L
            for r in range(L):
                start = (r // (2 * m)) * 2 * m
                mid = start + m
                upper = r >= mid
                if fwd:
                    if upper:
                        dmat[dr, base + r, mid:r + 1] = 1.0
                    else:
                        dmat[dr, base + r, r + 1:mid] = 1.0
                else:
                    if not upper:
                        dmat[dr, base + r, r:mid] = 1.0
                    else:
                        dmat[dr, base + r, mid:r] = 1.0
                qrole[dr, li, r, :] = 1.0 if (upper == fwd) else 0.0
            same = (t[:, None] // (2 * m)) == (t[None, :] // (2 * m))
            t_up = (t[:, None] % (2 * m)) >= m
            s_up = (t[None, :] % (2 * m)) >= m
            if fwd:
                smask[dr, li] = same & t_up & ~s_up
            else:
                smask[dr, li] = same & ~t_up & s_up
        smask[dr, A_LEVELS] = np.eye(L)
    return dmat, smask, qrole


def _hgrn_kernel(qf_ref, lff_ref, kf_ref, vf_ref, qb_ref, lfb_ref, kb_ref, vb_ref,
                 dmat_ref, smask_ref, qrole_ref, of_ref, ob_ref, stf_ref, stb_ref, *, n_chunks):
    L = A_CHUNK

    @pl.when(pl.program_id(1) == 0)
    def _():
        stf_ref[...] = jnp.zeros_like(stf_ref)
        stb_ref[...] = jnp.zeros_like(stb_ref)

    dirs = ((0, qf_ref, lff_ref, kf_ref, vf_ref, of_ref, stf_ref),
            (1, qb_ref, lfb_ref, kb_ref, vb_ref, ob_ref, stb_ref))

    def body(i, carry):
        chains = []
        for dr, q_ref, lf_ref, k_ref, v_ref, o_ref, st_ref in dirs:
            c = i if dr == 0 else n_chunks - 1 - i
            rows = pl.ds(pl.multiple_of(c * L, L), L)
            dm = dmat_ref[dr]
            for hp in range(A_HEADS // 2):
                g_hi, g_lo = _split_bf16(lf_ref[0, rows, 2 * hp * A_DK:(2 * hp + 2) * A_DK])
                ex2 = jnp.exp(_dot(dm, g_hi) + _dot(dm, g_lo))
                for hh in range(2):
                    h = 2 * hp + hh
                    chains.append((dr, h, rows, ex2[:, hh * A_DK:(hh + 1) * A_DK],
                                   q_ref, k_ref, v_ref, o_ref, st_ref))
        staged = []
        for dr, h, rows, ex, q_ref, k_ref, v_ref, o_ref, st_ref in chains:
            cols = slice(h * A_DK, (h + 1) * A_DK)
            q = q_ref[0, rows, cols].astype(F32)
            k = k_ref[0, rows, cols].astype(F32)
            vb = v_ref[0, rows, cols]
            eb = ex[0:L]
            ebl = eb[L - 1:L] if dr == 0 else eb[0:1]
            st = st_ref[h]
            o = _dot_nt((q * eb).astype(BF16), st.astype(BF16))
            st_ref[h] = st * ebl + _dot_tn(vb, (k * ex[L:2 * L]).astype(BF16))
            parts = [jnp.sum(q * k, axis=-1, keepdims=True)]
            for li in range(A_LEVELS):
                role = qrole_ref[dr, li] > 0.5
                xl = (jnp.where(role, q, k) * ex[(2 + li) * L:(3 + li) * L]).astype(BF16)
                parts.append(_dot_nt(xl, xl))
            staged.append((dr, rows, cols, o, parts, vb, o_ref))
        for dr, rows, cols, o, parts, vb, o_ref in staged:
            scores = parts[0] * smask_ref[dr, A_LEVELS]
            for li in range(A_LEVELS):
                scores = scores + parts[1 + li] * smask_ref[dr, li]
            o_ref[0, rows, cols] = o + _dot(scores.astype(BF16), vb)
        return carry

    lax.fori_loop(0, n_chunks, body, 0)


def _hgrn(aq, alff, akf, alfb, akb, av, tables, n_ctx):
    b, s, w = aq.shape
    ts = TOKEN_TILE
    n_tiles = s // ts
    n_ctx_tiles = n_ctx // ts
    dmat, smask, qrole = tables

    def bwd_tile(j):
        return jnp.where(j < n_ctx_tiles, n_ctx_tiles - 1 - j, n_tiles - 1 - (j - n_ctx_tiles))

    fwd = lambda: pl.BlockSpec((1, ts, w), lambda i, j: (i, j, 0))
    bwd = lambda: pl.BlockSpec((1, ts, w), lambda i, j: (i, bwd_tile(j), 0))
    const = lambda a: _resident(a.shape, lambda i, j: (0,) * a.ndim)
    state = pltpu.VMEM((A_HEADS, A_DK, A_DK), F32)
    return pl.pallas_call(
        functools.partial(_hgrn_kernel, n_chunks=ts // A_CHUNK),
        grid=(b, n_tiles),
        in_specs=[fwd(), fwd(), fwd(), fwd(), bwd(), bwd(), bwd(), bwd(),
                  const(dmat), const(smask), const(qrole)],
        out_specs=[fwd(), bwd()],
        out_shape=[jax.ShapeDtypeStruct((b, s, w), F32)] * 2,
        scratch_shapes=[state, state],
        compiler_params=_params("parallel", "arbitrary"),
        name="hgrn2_scan",
    )(aq, alff, akf, av, aq, alfb, akb, av, dmat, smask, qrole)


def _gqa_kernel(q_ref, k_ref, v_ref, o_ref, qs_ref, kmax_ref, *, n_ctx, s):
    tq = Q_TILE
    tk = KV_TILE
    j = pl.program_id(1)
    is_latent = j >= n_ctx // tq
    n_lat = (s - n_ctx) // tk
    lane = lax.broadcasted_iota(jnp.int32, (tq, LANES), 1)
    upper = lane >= HEAD_DIM
    ctx_rows = pl.ds(0, n_ctx)

    @pl.when(j == 0)
    def _():
        li = lax.broadcasted_iota(jnp.int32, (LANES, LANES), 0) // HEAD_DIM
        lj = lax.broadcasted_iota(jnp.int32, (LANES, LANES), 1) // HEAD_DIM
        head_sum = jnp.where(li == lj, 1.0, 0.0).astype(BF16)

        def body(c, mx):
            kk = k_ref[0, pl.ds(pl.multiple_of(c * tk, tk), tk), :].astype(F32)
            hi, lo = _split_bf16(kk * kk)
            return jnp.maximum(mx, _dot(hi, head_sum) + _dot(lo, head_sum))

        mx = lax.fori_loop(0, n_lat, body, jnp.zeros((tk, LANES), F32))
        kk = k_ref[0, pl.ds(s - n_ctx, n_ctx), :].astype(F32)
        hi, lo = _split_bf16(kk * kk)
        tail = _dot(hi, head_sum) + _dot(lo, head_sum)
        kmax_ref[...] = jnp.maximum(jnp.max(mx, axis=0, keepdims=True),
                                    jnp.max(tail, axis=0, keepdims=True))

    def lat_rows(c):
        return pl.ds(n_ctx + c * tk, tk)

    def scores(rows):
        return _dot_nt(qs_ref[...], k_ref[0, rows, :])

    def lane_max(sc):
        out = sc[:, 0:LANES]
        for jb in range(1, sc.shape[1] // LANES):
            out = jnp.maximum(out, sc[:, jb * LANES:(jb + 1) * LANES])
        return out

    def row_max(lane_wise):
        return jnp.broadcast_to(jnp.max(lane_wise, axis=-1, keepdims=True), lane_wise.shape)

    def probs(sc, mb):
        ps = []
        ls = None
        for jb in range(sc.shape[1] // LANES):
            pj = jnp.exp2(sc[:, jb * LANES:(jb + 1) * LANES] - mb)
            ls = pj if ls is None else ls + pj
            ps.append(pj.astype(BF16))
        return jnp.concatenate(ps, axis=-1), ls

    def finish(kv, ls, acc):
        out = acc * (1.0 / jnp.sum(ls, axis=-1, keepdims=True))
        pair_out = [None, None]
        for gi in range(B_GROUP):
            h = kv * B_GROUP + gi
            oh = out[gi * tq:(gi + 1) * tq, :]
            if h % 2 != kv:
                oh = pltpu.roll(oh, HEAD_DIM, 1)
            pair_out[h % 2] = oh
            if h % 2 == 1:
                o_ref[0, :, (h // 2) * LANES:(h // 2 + 1) * LANES] = jnp.where(
                    upper, pair_out[1], pair_out[0]).astype(BF16)

    def context_only(kv):
        sc = scores(ctx_rows)
        p, ls = probs(sc, row_max(lane_max(sc)))
        finish(kv, ls, _dot(p, v_ref[0, ctx_rows, :]))

    def exact_row_max():
        def body(c, mx):
            rows = pl.ds(pl.multiple_of(n_ctx + c * tk, LANES), tk)
            return jnp.maximum(mx, lane_max(scores(rows)))

        return row_max(lax.fori_loop(0, n_lat, body, lane_max(scores(ctx_rows))))

    def all_keys(kv, shift_fn):
        mb = shift_fn()
        sc_next = scores(ctx_rows)
        ls = None
        acc = None
        for c in range(-1, n_lat):
            rows = ctx_rows if c < 0 else lat_rows(c)
            sc = sc_next
            if c + 1 < n_lat:
                sc_next = scores(lat_rows(c + 1))
            p, lt = probs(sc, mb)
            pv = _dot(p, v_ref[0, rows, :])
            ls = lt if ls is None else ls + lt
            acc = pv if acc is None else acc + pv
        finish(kv, ls, acc)

    for kv in range(B_KV_HEADS):
        for gi in range(B_GROUP):
            h = kv * B_GROUP + gi
            blk = q_ref[0, :, (h // 2) * LANES:(h // 2 + 1) * LANES].astype(F32)
            if h % 2 != kv:
                blk = pltpu.roll(blk, HEAD_DIM, 1)
            keep = upper if kv == 1 else jnp.logical_not(upper)
            qs_ref[gi * tq:(gi + 1) * tq, :] = jnp.where(keep, blk, 0.0).astype(BF16)
        qsq = qs_ref[...].astype(F32)
        qn2 = jnp.sum(qsq * qsq, axis=-1, keepdims=True)
        kn2 = jnp.max(kmax_ref[:, kv * HEAD_DIM:(kv + 1) * HEAD_DIM], axis=-1, keepdims=True)
        bound = jnp.broadcast_to(jnp.sqrt(qn2 * kn2) * BOUND_SLACK, (B_GROUP * tq, LANES))
        bound_ok = jnp.max(bound) <= MAX_SAFE_SHIFT
        pl.when(jnp.logical_and(is_latent, bound_ok))(
            functools.partial(all_keys, kv, lambda: bound))
        pl.when(jnp.logical_and(is_latent, jnp.logical_not(bound_ok)))(
            functools.partial(all_keys, kv, exact_row_max))
        pl.when(jnp.logical_not(is_latent))(functools.partial(context_only, kv))


def _gqa(bq, bk, bv, n_ctx):
    b, s, w = bq.shape
    tq = Q_TILE
    return pl.pallas_call(
        functools.partial(_gqa_kernel, n_ctx=n_ctx, s=s),
        grid=(b, s // tq),
        in_specs=[
            pl.BlockSpec((1, tq, w), lambda i, j: (i, j, 0)),
            pl.BlockSpec((1, s, LANES), lambda i, j: (i, 0, 0)),
            pl.BlockSpec((1, s, LANES), lambda i, j: (i, 0, 0)),
        ],
        out_specs=pl.BlockSpec((1, tq, w), lambda i, j: (i, j, 0)),
        out_shape=jax.ShapeDtypeStruct((b, s, w), BF16),
        scratch_shapes=[pltpu.VMEM((B_GROUP * tq, LANES), BF16), pltpu.VMEM((1, LANES), F32)],
        compiler_params=_params("parallel", "arbitrary"),
        name="gqa_attention",
    )(bq, bk, bv)


def _natten_bias(rel_bias):
    qcol = np.arange(GRID_W)
    c0 = np.clip(qcol - WIN_C // 2, 0, GRID_W - WIN_C)
    kcol = np.arange(GRID_W)
    valid = (kcol[None, :] >= c0[:, None]) & (kcol[None, :] < c0[:, None] + WIN_C)
    dc = np.clip(kcol[None, :] - qcol[:, None] + (WIN_C - 1), 0, 2 * WIN_C - 2)
    cls = np.arange(WIN_R)
    jrow = np.arange(WIN_R)
    dr = jrow[None, :] - cls[:, None] + (WIN_R - 1)
    tab = rel_bias.astype(F32)[:, dr[:, None, :, None], dc[None, :, None, :]]
    tab = jnp.where(valid[None, None, :, None, :], tab, NEG_BIG)
    return tab.reshape(rel_bias.shape[0], WIN_R, GRID_W, WIN_R * GRID_W)


def _natten_kernel(q_ref, k_ref, v_ref, bias_ref, o_ref, *, n_ctx, grid_rows):
    w = GRID_W
    band = WIN_R * w
    lane = lax.broadcasted_iota(jnp.int32, (w, LANES), 1)
    upper = lane >= HEAD_DIM
    kc = k_ref[0, 0:n_ctx, :]
    vc = v_ref[0, 0:n_ctx, :]

    def heads(q, fn):
        outs = []
        for hh in range(2):
            keep = upper if hh == 1 else jnp.logical_not(upper)
            outs.append(fn(hh, jnp.where(keep, q, jnp.zeros_like(q))))
        return jnp.where(upper, outs[1], outs[0])

    def ctx_block(i, carry):
        rows = pl.ds(pl.multiple_of(i * w, w), w)

        def attend(hh, qh):
            sc = _dot_nt(qh, kc)
            p = jnp.exp(sc - jnp.max(sc, axis=-1, keepdims=True))
            return _dot(p.astype(BF16), vc) / jnp.sum(p, axis=-1, keepdims=True)

        o_ref[0, rows, :] = heads(q_ref[0, rows, :], attend).astype(BF16)
        return carry

    lax.fori_loop(0, n_ctx // w, ctx_block, 0)

    nr = NAT_ROWS

    def row_group(gi, carry):
        qrows = pl.ds(pl.multiple_of(n_ctx + gi * (nr * w), nr * w), nr * w)
        qg = q_ref[0, qrows, :]
        lane_g = lax.broadcasted_iota(jnp.int32, qg.shape, 1) >= HEAD_DIM
        qh = (jnp.where(lane_g, jnp.zeros_like(qg), qg), jnp.where(lane_g, qg, jnp.zeros_like(qg)))
        s_ctx = [_dot_nt(qh[hh], kc) for hh in range(2)]
        s_win = {}
        vbs = []
        for rr in range(nr):
            r = gi * nr + rr
            r0 = jnp.clip(r - WIN_R // 2, 0, grid_rows - WIN_R)
            cls = r - r0
            krows = pl.ds(pl.multiple_of(n_ctx + r0 * w, w), band)
            kb = k_ref[0, krows, :]
            vbs.append(v_ref[0, krows, :])
            for hh in range(2):
                s_win[rr, hh] = _dot_nt(qh[hh][rr * w:(rr + 1) * w], kb) + bias_ref[hh, cls]
        p_win = {}
        p_ctx = [[], []]
        den = {}
        for rr in range(nr):
            for hh in range(2):
                sw = s_win[rr, hh]
                sc = s_ctx[hh][rr * w:(rr + 1) * w]
                m = jnp.maximum(jnp.max(sw, axis=-1, keepdims=True),
                                jnp.max(sc, axis=-1, keepdims=True))
                pw = jnp.exp(sw - m)
                pc = jnp.exp(sc - m)
                den[rr, hh] = jnp.sum(pw, axis=-1, keepdims=True) + jnp.sum(pc, axis=-1, keepdims=True)
                p_win[rr, hh] = pw.astype(BF16)
                p_ctx[hh].append(pc.astype(BF16))
        o_ctx = [_dot(jnp.concatenate(p_ctx[hh], axis=0), vc) for hh in range(2)]
        o_win = {(rr, hh): _dot(p_win[rr, hh], vbs[rr]) for rr in range(nr) for hh in range(2)}
        for rr in range(nr):
            outs = [(o_win[rr, hh] + o_ctx[hh][rr * w:(rr + 1) * w]) / den[rr, hh] for hh in range(2)]
            rows = pl.ds(pl.multiple_of(n_ctx + (gi * nr + rr) * w, w), w)
            o_ref[0, rows, :] = jnp.where(upper, outs[1], outs[0]).astype(BF16)
        return carry

    lax.fori_loop(0, grid_rows // nr, row_group, 0)


def _natten(cq, ck, cv, bias, n_ctx):
    b, s, w = cq.shape
    grid_rows = (s - n_ctx) // GRID_W
    assert grid_rows >= WIN_R and grid_rows % NAT_ROWS == 0
    blk = lambda: pl.BlockSpec((1, s, LANES), lambda i, p: (i, 0, p))
    return pl.pallas_call(
        functools.partial(_natten_kernel, n_ctx=n_ctx, grid_rows=grid_rows),
        grid=(b, w // LANES),
        in_specs=[blk(), blk(), blk(),
                  pl.BlockSpec((2, WIN_R, GRID_W, WIN_R * GRID_W), lambda i, p: (p, 0, 0, 0))],
        out_specs=blk(),
        out_shape=jax.ShapeDtypeStruct((b, s, w), BF16),
        compiler_params=_params("parallel", "parallel"),
        name="neighborhood_attention",
    )(cq, ck, cv, bias)


def _merge_kernel(x_ref, oaf_ref, oab_ref, og_ref, yb_ref, yc_ref, gt_ref, mod_ref, gn_ref, wbr_ref, wo_ref,
                  o_ref, *, d):
    ya = []
    for h in range(A_HEADS):
        sl = slice(h * A_DK, (h + 1) * A_DK)
        oh = oaf_ref[0, :, sl] + oab_ref[0, :, sl]
        yh = oh * lax.rsqrt(jnp.mean(oh * oh, axis=-1, keepdims=True) + EPS) * gn_ref[...]
        ya.append((yh * _silu(og_ref[0, :, sl].astype(F32))).astype(BF16))
    ys = (jnp.concatenate(ya, axis=-1), yb_ref[0], yc_ref[0])
    merged = None
    for k in range(N_BRANCH):
        term = gt_ref[0, :, k * d:(k + 1) * d].astype(F32) * _dot(ys[k], wbr_ref[k])
        merged = term if merged is None else merged + term
    g1 = mod_ref[0, 0, :, 2 * d:3 * d]
    o_ref[0] = x_ref[0] + g1 * _dot(merged.astype(BF16), wo_ref[...])


def _merge(xs, oaf, oab, og, yb, yc, gt, modsel, gn, wbr, wo, n_ctx_tiles, tile0, n_tiles):
    b, _, d = xs.shape
    tm = TOKEN_TILE
    tok = lambda w: pl.BlockSpec((1, tm, w), lambda i, j: (i, j + tile0, 0))
    return pl.pallas_call(
        functools.partial(_merge_kernel, d=d),
        grid=(b, n_tiles),
        in_specs=[
            tok(d), tok(oaf.shape[-1]), tok(oab.shape[-1]), tok(og.shape[-1]), tok(yb.shape[-1]), tok(yc.shape[-1]),
            tok(gt.shape[-1]),
            pl.BlockSpec((1, 1, 1, modsel.shape[-1]),
                         lambda i, j: (i, (j + tile0 >= n_ctx_tiles).astype(jnp.int32), 0, 0)),
            _resident((1, A_DK), lambda i, j: (0, 0)),
            _resident(wbr.shape, lambda i, j: (0, 0, 0)),
            _resident(wo.shape, lambda i, j: (0, 0)),
        ],
        out_specs=pl.BlockSpec((1, tm, d), lambda i, j: (i, j, 0)),
        out_shape=jax.ShapeDtypeStruct((b, n_tiles * tm, d), F32),
        compiler_params=_params("parallel", "parallel"),
        name="merge_out_projection",
    )(xs, oaf, oab, og, yb, yc, gt, modsel, gn, wbr, wo)


def _ffn_kernel(x_ref, mod_ref, nw_ref, wgu_ref, wd_ref, o_ref, *, d, hidden, n_split):
    x = x_ref[0]
    y = x * lax.rsqrt(jnp.mean(x * x, axis=-1, keepdims=True) + EPS) * nw_ref[...]
    sh2 = mod_ref[0, 0, :, 3 * d:4 * d]
    sc2 = mod_ref[0, 0, :, 4 * d:5 * d]
    g2 = mod_ref[0, 0, :, 5 * d:6 * d]
    h = (y * (1.0 + sc2) + sh2).astype(BF16)
    step = hidden // n_split
    acc = None
    for c in range(n_split):
        a = _dot(h, wgu_ref[:, c * step:(c + 1) * step])
        g = _dot(h, wgu_ref[:, hidden + c * step:hidden + (c + 1) * step])
        part = _dot((_silu(a) * g).astype(BF16), wd_ref[c * step:(c + 1) * step, :])
        acc = part if acc is None else acc + part
    o_ref[0] = x + g2 * acc


def _ffn(xs, modsel, norm_w, wgu, wd, n_ctx_tiles, tile0):
    b, s, d = xs.shape
    tm = TOKEN_TILE
    hidden = wd.shape[0]
    n_split = 2
    assert hidden % (n_split * LANES) == 0
    return pl.pallas_call(
        functools.partial(_ffn_kernel, d=d, hidden=hidden, n_split=n_split),
        grid=(b, s // tm),
        in_specs=[
            pl.BlockSpec((1, tm, d), lambda i, j: (i, j, 0)),
            pl.BlockSpec((1, 1, 1, modsel.shape[-1]),
                         lambda i, j: (i, (j + tile0 >= n_ctx_tiles).astype(jnp.int32), 0, 0)),
            _resident((1, d), lambda i, j: (0, 0)),
            _resident(wgu.shape, lambda i, j: (0, 0)),
            _resident(wd.shape, lambda i, j: (0, 0)),
        ],
        out_specs=pl.BlockSpec((1, tm, d), lambda i, j: (i, j, 0)),
        out_shape=jax.ShapeDtypeStruct((b, s, d), F32),
        compiler_params=_params("parallel", "parallel"),
        name="swiglu_ffn",
    )(xs, modsel, norm_w, wgu, wd)


def _rope_tables(t, n_ctx):
    pos = jnp.arange(t)
    row = (pos // GRID_W).astype(F32)
    colp = (pos % GRID_W).astype(F32)
    n = HEAD_DIM // 4
    inv = ROPE_THETA ** (-jnp.arange(n, dtype=F32) / n)
    ang = jnp.concatenate([row[:, None] * inv, colp[:, None] * inv], axis=-1)
    cos = jnp.cos(ang)
    sin = jnp.sin(ang)
    reps = LANES // HEAD_DIM
    cos_t = jnp.tile(jnp.concatenate([cos, cos], axis=-1), (1, reps))
    sin_t = jnp.tile(jnp.concatenate([-sin, sin], axis=-1), (1, reps))
    cos_t = jnp.concatenate([jnp.ones((n_ctx, LANES), F32), cos_t], axis=0)
    sin_t = jnp.concatenate([jnp.zeros((n_ctx, LANES), F32), sin_t], axis=0)
    return cos_t, sin_t


def _deinterleave_heads(a):
    lead = a.shape[:-1]
    n = a.shape[-1] // HEAD_DIM
    a = a.reshape(lead + (n, HEAD_DIM // 2, 2))
    return jnp.swapaxes(a, -1, -2).reshape(lead + (n * HEAD_DIM,))


def _inproj_weight(w):
    b0 = 5 * A_HEADS * A_DK
    b1 = b0 + (B_HEADS + B_KV_HEADS) * HEAD_DIM
    return jnp.concatenate([w[:, :b0], _deinterleave_heads(w[:, b0:b1]), w[:, b1:]],
                           axis=1).astype(BF16)


def kernel(x, c, ctx, c_ctx, w_mod, b_mod, norm_mix, norm_ffn, w_in, lb_raw, gn_a, qn_b, kn_b,
           qn_c, kn_c, rel_bias, w_branch, w_out, w_gate_up, w_down):
    bn, t, d = x.shape
    n_ctx = ctx.shape[1]
    depth = w_mod.shape[0]
    tm = TOKEN_TILE
    assert n_ctx % tm == 0 and t % tm == 0 and t % GRID_W == 0
    n_ctx_tiles = n_ctx // tm
    aw = A_HEADS * A_DK

    c_all = jnp.zeros((8, d), F32).at[:bn].set(c).at[bn].set(c_ctx)
    mod = _modulation(c_all, w_mod, b_mod)
    modsel = jnp.stack([jnp.broadcast_to(mod[:, bn:bn + 1], (depth, bn, 6 * d)), mod[:, :bn]],
                       axis=2)[:, :, :, None, :]

    lbp = jax.nn.softmax(lb_raw.astype(F32), axis=0)
    lb_all = jnp.clip(jnp.cumsum(lbp, axis=0) - lbp[:1], 0.0, 1.0 - 1e-6)
    zeros2 = jnp.zeros_like(lb_all)
    lbc_all = jnp.concatenate([jnp.log(lb_all + LB_TINY), jnp.log1p(-lb_all), 1.0 - lb_all, zeros2],
                              axis=1)

    cos_t, sin_t = _rope_tables(t, n_ctx)
    lane = np.arange(LANES)
    gmat = jnp.asarray((lane[:, None] // HEAD_DIM == lane[None, :] // HEAD_DIM) / HEAD_DIM, BF16)
    dmat, smask, qrole = _hgrn_tables()
    tables = (jnp.asarray(dmat, BF16), jnp.asarray(smask), jnp.asarray(qrole))

    xs = jnp.concatenate([ctx, x], axis=1)
    for l in range(depth):
        last = l == depth - 1
        w_bf = _inproj_weight(w_in[l])
        headw = jnp.stack([jnp.tile(_deinterleave_heads(qn_b[l]), B_HEADS),
                           jnp.tile(_deinterleave_heads(kn_b[l]), B_HEADS),
                           jnp.tile(qn_c[l], C_HEADS), jnp.tile(kn_c[l], C_HEADS)]).astype(F32)
        headw = jnp.concatenate([headw, jnp.zeros((4, headw.shape[1]), F32)], axis=0)
        (aq, alff, akf, alfb, akb, av, aog, bq, bk, bv, cq, ck, cv, gt) = _in_projection(
            xs, modsel[l], norm_mix[l][None, :], w_bf, lbc_all[l], headw, cos_t, sin_t, gmat,
            n_ctx_tiles)
        oaf, oab = _hgrn(aq, alff, akf, alfb, akb, av, tables, n_ctx)
        yb = _gqa(bq, bk, bv, n_ctx)
        yc = _natten(cq, ck, cv, _natten_bias(rel_bias[l]), n_ctx)
        tile0 = n_ctx_tiles if last else 0
        n_tiles = (t if last else t + n_ctx) // tm
        xm = _merge(xs, oaf, oab, aog, yb, yc, gt, modsel[l], gn_a[l][None, :],
                    w_branch[l].astype(BF16), w_out[l].astype(BF16), n_ctx_tiles, tile0, n_tiles)
        xs = _ffn(xm, modsel[l], norm_ffn[l][None, :], w_gate_up[l].astype(BF16),
                  w_down[l].astype(BF16), n_ctx_tiles, tile0)
    return xs
```

```python
import functools

import numpy as np
import jax
import jax.numpy as jnp
from jax import lax
from jax.experimental import pallas as pl
from jax.experimental.pallas import tpu as pltpu

GRID_W = 64
A_HEADS = 4
A_DK = 128
B_HEADS = 8
B_KV_HEADS = 2
B_GROUP = B_HEADS // B_KV_HEADS
HEAD_DIM = 64
C_HEADS = 8
WIN_R = 8
WIN_C = 16
N_BRANCH = 3
ROPE_THETA = 10000.0
EPS = 1e-6
LB_TINY = 1e-30

LANES = 128
TOKEN_TILE = 256
A_CHUNK = 64
A_LEVELS = 6
KV_TILE = 512
Q_TILE = 128
NAT_ROWS = 4
NEG_BIG = -1e30
LOG2_E = 1.4426950408889634
BOUND_SLACK = 1.02
MAX_SAFE_SHIFT = 50.0
VMEM_LIMIT = 56 * 1024 * 1024

F32 = jnp.float32
BF16 = jnp.bfloat16


def _dot(a, b):
    return jnp.dot(a, b, preferred_element_type=F32)


def _dot_nt(a, b):
    return lax.dot_general(a, b, (((1,), (1,)), ((), ())), preferred_element_type=F32)


def _dot_tn(a, b):
    return lax.dot_general(a, b, (((0,), (0,)), ((), ())), preferred_element_type=F32)


def _split_bf16(x):
    hi = x.astype(BF16)
    lo = (x - hi.astype(F32)).astype(BF16)
    return hi, lo


def _silu(x):
    return x * jax.nn.sigmoid(x)


def _params(*sem):
    return pltpu.CompilerParams(dimension_semantics=sem, vmem_limit_bytes=VMEM_LIMIT)


def _resident(shape, index_map):
    return pl.BlockSpec(shape, index_map, pipeline_mode=pl.Buffered(1))


def _mod_kernel(c_ref, w_ref, b_ref, o_ref):
    a = _silu(c_ref[...])
    a_hi, a_lo = _split_bf16(a)
    w = w_ref[0]
    w_hi, w_lo = _split_bf16(w)
    acc = _dot(a_hi, w_hi) + _dot(a_hi, w_lo) + _dot(a_lo, w_hi)
    o_ref[0] = acc + b_ref[0]


def _modulation(c_all, w_mod, b_mod):
    depth, d, n = w_mod.shape
    rows = c_all.shape[0]
    tn = 1536
    return pl.pallas_call(
        _mod_kernel,
        grid=(depth, n // tn),
        in_specs=[
            pl.BlockSpec((rows, d), lambda l, j: (0, 0)),
            pl.BlockSpec((1, d, tn), lambda l, j: (l, 0, j)),
            pl.BlockSpec((1, 1, tn), lambda l, j: (l, 0, j)),
        ],
        out_specs=pl.BlockSpec((1, rows, tn), lambda l, j: (l, 0, j)),
        out_shape=jax.ShapeDtypeStruct((depth, rows, n), F32),
        compiler_params=_params("arbitrary", "arbitrary"),
        name="modulation",
    )(c_all, w_mod, b_mod.reshape(depth, 1, n))


def _group_mean_sq(x, gmat):
    hi, lo = _split_bf16(x * x)
    return _dot(hi, gmat) + _dot(lo, gmat)


def _head_norm(x, gmat, w):
    return x * lax.rsqrt(_group_mean_sq(x, gmat) + EPS) * w


def _rope(x, cos, sin_signed, first_half):
    rot = jnp.where(first_half, pltpu.roll(x, LANES - HEAD_DIM // 2, 1),
                    pltpu.roll(x, HEAD_DIM // 2, 1))
    return x * cos + rot * sin_signed


def _log_forget(z, la, l1):
    ls = jnp.minimum(z, 0.0) - jnp.log1p(jnp.exp(-jnp.abs(z)))
    t = l1 + ls
    return jnp.maximum(la, t) + jnp.log1p(jnp.exp(-jnp.abs(la - t)))


def _inproj_kernel(x_ref, mod_ref, nw_ref, w_ref, lbc_ref, hw_ref, cos_ref, sin_ref, gmat_ref,
                   aq_ref, alff_ref, akf_ref, alfb_ref, akb_ref, av_ref, aog_ref,
                   bq_ref, bk_ref, bv_ref, cq_ref, ck_ref, cv_ref, gt_ref, *, d):
    x = x_ref[0]
    ms = jnp.mean(x * x, axis=-1, keepdims=True)
    y = x * lax.rsqrt(ms + EPS) * nw_ref[...]
    sh1 = mod_ref[0, 0, :, 0:d]
    sc1 = mod_ref[0, 0, :, d:2 * d]
    u = (y * (1.0 + sc1) + sh1).astype(BF16)

    aw = A_HEADS * A_DK
    col = [0]

    def proj(width):
        c0 = col[0]
        col[0] = c0 + width
        return _dot(u, w_ref[:, c0:c0 + width])

    aq_ref[0] = (proj(aw) * (A_DK ** -0.5)).astype(BF16)
    for lf_ref, k_ref, row in ((alff_ref, akf_ref, 0), (alfb_ref, akb_ref, 1)):
        z = proj(aw)
        la = lbc_ref[row:row + 1, :]
        l1 = lbc_ref[2 + row:3 + row, :]
        oml = lbc_ref[4 + row:5 + row, :]
        lf_ref[0] = _log_forget(z, la, l1)
        k_ref[0] = (oml * jax.nn.sigmoid(-z)).astype(BF16)
    av_ref[0] = proj(aw).astype(BF16)
    aog_ref[0] = proj(aw).astype(BF16)

    gmat = gmat_ref[...]
    lane = lax.broadcasted_iota(jnp.int32, (x.shape[0], LANES), 1)
    first_half = (lane % HEAD_DIM) < (HEAD_DIM // 2)
    cos = cos_ref[...]
    sin = sin_ref[...]
    scale = HEAD_DIM ** -0.5

    zq = proj(B_HEADS * HEAD_DIM)
    for j in range(B_HEADS * HEAD_DIM // LANES):
        sl = slice(j * LANES, (j + 1) * LANES)
        qn = _head_norm(zq[:, sl], gmat, hw_ref[0:1, sl])
        bq_ref[0, :, sl] = (_rope(qn, cos, sin, first_half) * (scale * LOG2_E)).astype(BF16)
    zk = proj(B_KV_HEADS * HEAD_DIM)
    kn = _head_norm(zk, gmat, hw_ref[1:2, 0:LANES])
    bk_ref[0] = _rope(kn, cos, sin, first_half).astype(BF16)
    bv_ref[0] = proj(B_KV_HEADS * HEAD_DIM).astype(BF16)

    zq = proj(C_HEADS * HEAD_DIM)
    for j in range(C_HEADS * HEAD_DIM // LANES):
        sl = slice(j * LANES, (j + 1) * LANES)
        cq_ref[0, :, sl] = (_head_norm(zq[:, sl], gmat, hw_ref[2:3, sl]) * scale).astype(BF16)
    zk = proj(C_HEADS * HEAD_DIM)
    for j in range(C_HEADS * HEAD_DIM // LANES):
        sl = slice(j * LANES, (j + 1) * LANES)
        ck_ref[0, :, sl] = _head_norm(zk[:, sl], gmat, hw_ref[3:4, sl]).astype(BF16)
    cv_ref[0] = proj(C_HEADS * HEAD_DIM).astype(BF16)

    for k in range(N_BRANCH):
        gt_ref[0, :, k * d:(k + 1) * d] = jax.nn.sigmoid(proj(d)).astype(BF16)


def _in_projection(xs, modsel, norm_w, w_bf, lbc, headw, cos_t, sin_t, gmat, n_ctx_tiles):
    b, s, d = xs.shape
    tm = TOKEN_TILE
    aw = A_HEADS * A_DK
    n_in = w_bf.shape[1]
    widths = [(aw, BF16), (aw, F32), (aw, BF16), (aw, F32), (aw, BF16), (aw, BF16), (aw, BF16),
              (B_HEADS * HEAD_DIM, BF16), (B_KV_HEADS * HEAD_DIM, BF16), (B_KV_HEADS * HEAD_DIM, BF16),
              (C_HEADS * HEAD_DIM, BF16), (C_HEADS * HEAD_DIM, BF16), (C_HEADS * HEAD_DIM, BF16),
              (N_BRANCH * d, BF16)]
    tok = lambda w: pl.BlockSpec((1, tm, w), lambda i, j: (i, j, 0))
    return pl.pallas_call(
        functools.partial(_inproj_kernel, d=d),
        grid=(b, s // tm),
        in_specs=[
            tok(d),
            pl.BlockSpec((1, 1, 1, modsel.shape[-1]),
                         lambda i, j: (i, (j >= n_ctx_tiles).astype(jnp.int32), 0, 0)),
            _resident((1, d), lambda i, j: (0, 0)),
            _resident((d, n_in), lambda i, j: (0, 0)),
            _resident(lbc.shape, lambda i, j: (0, 0)),
            _resident(headw.shape, lambda i, j: (0, 0)),
            pl.BlockSpec((tm, LANES), lambda i, j: (j, 0)),
            pl.BlockSpec((tm, LANES), lambda i, j: (j, 0)),
            _resident((LANES, LANES), lambda i, j: (0, 0)),
        ],
        out_specs=[tok(w) for w, _ in widths],
        out_shape=[jax.ShapeDtypeStruct((b, s, w), dt) for w, dt in widths],
        compiler_params=_params("parallel", "parallel"),
        name="in_projection",
    )(xs, modsel, norm_w, w_bf, lbc, headw, cos_t, sin_t, gmat)


def _hgrn_tables():
    L = A_CHUNK
    ms = [1 << i for i in range(A_LEVELS)]
    dmat = np.zeros((2, (2 + A_LEVELS) * L, L), np.float32)
    smask = np.zeros((2, A_LEVELS + 1, L, L), np.float32)
    qrole = np.zeros((2, A_LEVELS, L, LANES), np.float32)
    t = np.arange(L)
    for dr in range(2):
        fwd = dr == 0
        for r in range(L):
            if fwd:
                dmat[dr, r, :r + 1] = 1.0
                dmat[dr, L + r, r + 1:] = 1.0
            else:
                dmat[dr, r, r:] = 1.0
                dmat[dr, L + r, :r] = 1.0
        for li, m in enumerate(ms):
            base = (2 + li) * L
            for r in range(L):
                start = (r // (2 * m)) * 2 * m
                mid = start + m
                upper = r >= mid
                if fwd:
                    if upper:
                        dmat[dr, base + r, mid:r + 1] = 1.0
                    else:
                        dmat[dr, base + r, r + 1:mid] = 1.0
                else:
                    if not upper:
                        dmat[dr, base + r, r:mid] = 1.0
                    else:
                        dmat[dr, base + r, mid:r] = 1.0
                qrole[dr, li, r, :] = 1.0 if (upper == fwd) else 0.0
            same = (t[:, None] // (2 * m)) == (t[None, :] // (2 * m))
            t_up = (t[:, None] % (2 * m)) >= m
            s_up = (t[None, :] % (2 * m)) >= m
            if fwd:
                smask[dr, li] = same & t_up & ~s_up
            else:
                smask[dr, li] = same & ~t_up & s_up
        smask[dr, A_LEVELS] = np.eye(L)
    return dmat, smask, qrole


def _hgrn_kernel(qf_ref, lff_ref, kf_ref, vf_ref, qb_ref, lfb_ref, kb_ref, vb_ref,
                 dmat_ref, smask_ref, qrole_ref, of_ref, ob_ref, stf_ref, stb_ref, *, n_chunks):
    L = A_CHUNK

    @pl.when(pl.program_id(1) == 0)
    def _():
        stf_ref[...] = jnp.zeros_like(stf_ref)
        stb_ref[...] = jnp.zeros_like(stb_ref)

    dirs = ((0, qf_ref, lff_ref, kf_ref, vf_ref, of_ref, stf_ref),
            (1, qb_ref, lfb_ref, kb_ref, vb_ref, ob_ref, stb_ref))

    def body(i, carry):
        chains = []
        for dr, q_ref, lf_ref, k_ref, v_ref, o_ref, st_ref in dirs:
            c = i if dr == 0 else n_chunks - 1 - i
            rows = pl.ds(pl.multiple_of(c * L, L), L)
            dm = dmat_ref[dr]
            for hp in range(A_HEADS // 2):
                g_hi, g_lo = _split_bf16(lf_ref[0, rows, 2 * hp * A_DK:(2 * hp + 2) * A_DK])
                ex2 = jnp.exp(_dot(dm, g_hi) + _dot(dm, g_lo))
                for hh in range(2):
                    h = 2 * hp + hh
                    chains.append((dr, h, rows, ex2[:, hh * A_DK:(hh + 1) * A_DK],
                                   q_ref, k_ref, v_ref, o_ref, st_ref))
        staged = []
        for dr, h, rows, ex, q_ref, k_ref, v_ref, o_ref, st_ref in chains:
            cols = slice(h * A_DK, (h + 1) * A_DK)
            q = q_ref[0, rows, cols].astype(F32)
            k = k_ref[0, rows, cols].astype(F32)
            vb = v_ref[0, rows, cols]
            eb = ex[0:L]
            ebl = eb[L - 1:L] if dr == 0 else eb[0:1]
            st = st_ref[h]
            o = _dot_nt((q * eb).astype(BF16), st.astype(BF16))
            st_ref[h] = st * ebl + _dot_tn(vb, (k * ex[L:2 * L]).astype(BF16))
            parts = [jnp.sum(q * k, axis=-1, keepdims=True)]
            for li in range(A_LEVELS):
                role = qrole_ref[dr, li] > 0.5
                xl = (jnp.where(role, q, k) * ex[(2 + li) * L:(3 + li) * L]).astype(BF16)
                parts.append(_dot_nt(xl, xl))
            staged.append((dr, rows, cols, o, parts, vb, o_ref))
        for dr, rows, cols, o, parts, vb, o_ref in staged:
            scores = parts[0] * smask_ref[dr, A_LEVELS]
            for li in range(A_LEVELS):
                scores = scores + parts[1 + li] * smask_ref[dr, li]
            o_ref[0, rows, cols] = o + _dot(scores.astype(BF16), vb)
        return carry

    lax.fori_loop(0, n_chunks, body, 0)


def _hgrn(aq, alff, akf, alfb, akb, av, tables, n_ctx):
    b, s, w = aq.shape
    ts = TOKEN_TILE
    n_tiles = s // ts
    n_ctx_tiles = n_ctx // ts
    dmat, smask, qrole = tables

    def bwd_tile(j):
        return jnp.where(j < n_ctx_tiles, n_ctx_tiles - 1 - j, n_tiles - 1 - (j - n_ctx_tiles))

    fwd = lambda: pl.BlockSpec((1, ts, w), lambda i, j: (i, j, 0))
    bwd = lambda: pl.BlockSpec((1, ts, w), lambda i, j: (i, bwd_tile(j), 0))
    const = lambda a: _resident(a.shape, lambda i, j: (0,) * a.ndim)
    state = pltpu.VMEM((A_HEADS, A_DK, A_DK), F32)
    return pl.pallas_call(
        functools.partial(_hgrn_kernel, n_chunks=ts // A_CHUNK),
        grid=(b, n_tiles),
        in_specs=[fwd(), fwd(), fwd(), fwd(), bwd(), bwd(), bwd(), bwd(),
                  const(dmat), const(smask), const(qrole)],
        out_specs=[fwd(), bwd()],
        out_shape=[jax.ShapeDtypeStruct((b, s, w), F32)] * 2,
        scratch_shapes=[state, state],
        compiler_params=_params("parallel", "arbitrary"),
        name="hgrn2_scan",
    )(aq, alff, akf, av, aq, alfb, akb, av, dmat, smask, qrole)


def _gqa_kernel(q_ref, k_ref, v_ref, o_ref, qs_ref, kmax_ref, *, n_ctx, s):
    tq = Q_TILE
    tk = KV_TILE
    j = pl.program_id(1)
    is_latent = j >= n_ctx // tq
    n_lat = (s - n_ctx) // tk
    lane = lax.broadcasted_iota(jnp.int32, (tq, LANES), 1)
    upper = lane >= HEAD_DIM
    ctx_rows = pl.ds(0, n_ctx)

    @pl.when(j == 0)
    def _():
        li = lax.broadcasted_iota(jnp.int32, (LANES, LANES), 0) // HEAD_DIM
        lj = lax.broadcasted_iota(jnp.int32, (LANES, LANES), 1) // HEAD_DIM
        head_sum = jnp.where(li == lj, 1.0, 0.0).astype(BF16)

        def body(c, mx):
            kk = k_ref[0, pl.ds(pl.multiple_of(c * tk, tk), tk), :].astype(F32)
            hi, lo = _split_bf16(kk * kk)
            return jnp.maximum(mx, _dot(hi, head_sum) + _dot(lo, head_sum))

        mx = lax.fori_loop(0, n_lat, body, jnp.zeros((tk, LANES), F32))
        kk = k_ref[0, pl.ds(s - n_ctx, n_ctx), :].astype(F32)
        hi, lo = _split_bf16(kk * kk)
        tail = _dot(hi, head_sum) + _dot(lo, head_sum)
        kmax_ref[...] = jnp.maximum(jnp.max(mx, axis=0, keepdims=True),
                                    jnp.max(tail, axis=0, keepdims=True))

    def lat_rows(c):
        return pl.ds(n_ctx + c * tk, tk)

    def scores(rows):
        return _dot_nt(qs_ref[...], k_ref[0, rows, :])

    def lane_max(sc):
        out = sc[:, 0:LANES]
        for jb in range(1, sc.shape[1] // LANES):
            out = jnp.maximum(out, sc[:, jb * LANES:(jb + 1) * LANES])
        return out

    def row_max(lane_wise):
        return jnp.broadcast_to(jnp.max(lane_wise, axis=-1, keepdims=True), lane_wise.shape)

    def probs(sc, mb):
        ps = []
        ls = None
        for jb in range(sc.shape[1] // LANES):
            pj = jnp.exp2(sc[:, jb * LANES:(jb + 1) * LANES] - mb)
            ls = pj if ls is None else ls + pj
            ps.append(pj.astype(BF16))
        return jnp.concatenate(ps, axis=-1), ls

    def finish(kv, ls, acc):
        out = acc * (1.0 / jnp.sum(ls, axis=-1, keepdims=True))
        pair_out = [None, None]
        for gi in range(B_GROUP):
            h = kv * B_GROUP + gi
            oh = out[gi * tq:(gi + 1) * tq, :]
            if h % 2 != kv:
                oh = pltpu.roll(oh, HEAD_DIM, 1)
            pair_out[h % 2] = oh
            if h % 2 == 1:
                o_ref[0, :, (h // 2) * LANES:(h // 2 + 1) * LANES] = jnp.where(
                    upper, pair_out[1], pair_out[0]).astype(BF16)

    def context_only(kv):
        sc = scores(ctx_rows)
        p, ls = probs(sc, row_max(lane_max(sc)))
        finish(kv, ls, _dot(p, v_ref[0, ctx_rows, :]))

    def exact_row_max():
        def body(c, mx):
            rows = pl.ds(pl.multiple_of(n_ctx + c * tk, LANES), tk)
            return jnp.maximum(mx, lane_max(scores(rows)))

        return row_max(lax.fori_loop(0, n_lat, body, lane_max(scores(ctx_rows))))

    def all_keys(kv, shift_fn):
        mb = shift_fn()
        sc_next = scores(ctx_rows)
        ls = None
        acc = None
        for c in range(-1, n_lat):
            rows = ctx_rows if c < 0 else lat_rows(c)
            sc = sc_next
            if c + 1 < n_lat:
                sc_next = scores(lat_rows(c + 1))
            p, lt = probs(sc, mb)
            pv = _dot(p, v_ref[0, rows, :])
            ls = lt if ls is None else ls + lt
            acc = pv if acc is None else acc + pv
        finish(kv, ls, acc)

    for kv in range(B_KV_HEADS):
        for gi in range(B_GROUP):
            h = kv * B_GROUP + gi
            blk = q_ref[0, :, (h // 2) * LANES:(h // 2 + 1) * LANES].astype(F32)
            if h % 2 != kv:
                blk = pltpu.roll(blk, HEAD_DIM, 1)
            keep = upper if kv == 1 else jnp.logical_not(upper)
            qs_ref[gi * tq:(gi + 1) * tq, :] = jnp.where(keep, blk, 0.0).astype(BF16)
        qsq = qs_ref[...].astype(F32)
        qn2 = jnp.sum(qsq * qsq, axis=-1, keepdims=True)
        kn2 = jnp.max(kmax_ref[:, kv * HEAD_DIM:(kv + 1) * HEAD_DIM], axis=-1, keepdims=True)
        bound = jnp.broadcast_to(jnp.sqrt(qn2 * kn2) * BOUND_SLACK, (B_GROUP * tq, LANES))
        bound_ok = jnp.max(bound) <= MAX_SAFE_SHIFT
        pl.when(jnp.logical_and(is_latent, bound_ok))(
            functools.partial(all_keys, kv, lambda: bound))
        pl.when(jnp.logical_and(is_latent, jnp.logical_not(bound_ok)))(
            functools.partial(all_keys, kv, exact_row_max))
        pl.when(jnp.logical_not(is_latent))(functools.partial(context_only, kv))


def _gqa(bq, bk, bv, n_ctx):
    b, s, w = bq.shape
    tq = Q_TILE
    return pl.pallas_call(
        functools.partial(_gqa_kernel, n_ctx=n_ctx, s=s),
        grid=(b, s // tq),
        in_specs=[
            pl.BlockSpec((1, tq, w), lambda i, j: (i, j, 0)),
            pl.BlockSpec((1, s, LANES), lambda i, j: (i, 0, 0)),
            pl.BlockSpec((1, s, LANES), lambda i, j: (i, 0, 0)),
        ],
        out_specs=pl.BlockSpec((1, tq, w), lambda i, j: (i, j, 0)),
        out_shape=jax.ShapeDtypeStruct((b, s, w), BF16),
        scratch_shapes=[pltpu.VMEM((B_GROUP * tq, LANES), BF16), pltpu.VMEM((1, LANES), F32)],
        compiler_params=_params("parallel", "arbitrary"),
        name="gqa_attention",
    )(bq, bk, bv)


def _natten_bias(rel_bias):
    h = rel_bias.shape[0]
    qcol = np.arange(GRID_W)
    c0 = np.clip(qcol - WIN_C // 2, 0, GRID_W - WIN_C)
    kcol = np.arange(GRID_W)
    valid = (kcol[None, :] >= c0[:, None]) & (kcol[None, :] < c0[:, None] + WIN_C)
    dc = kcol[None, :] - qcol[:, None] + (WIN_C - 1)
    pick = (dc[None, :, :] == np.arange(2 * WIN_C - 1)[:, None, None]) & valid[None]
    cols = jnp.einsum('hdi,iqk->hdqk', rel_bias.astype(F32), jnp.asarray(pick, F32),
                      precision=lax.Precision.HIGHEST)
    cols = jnp.where(valid[None, None], cols, NEG_BIG)
    per_class = [jnp.swapaxes(cols[:, WIN_R - 1 - c:2 * WIN_R - 1 - c], 1, 2) for c in range(WIN_R)]
    return jnp.stack(per_class, axis=1).reshape(h, WIN_R, GRID_W, WIN_R * GRID_W)


def _natten_kernel(q_ref, k_ref, v_ref, bias_ref, o_ref, *, n_ctx, grid_rows):
    w = GRID_W
    band = WIN_R * w
    lane = lax.broadcasted_iota(jnp.int32, (w, LANES), 1)
    upper = lane >= HEAD_DIM
    kc = k_ref[0, 0:n_ctx, :]
    vc = v_ref[0, 0:n_ctx, :]

    def heads(q, fn):
        outs = []
        for hh in range(2):
            keep = upper if hh == 1 else jnp.logical_not(upper)
            outs.append(fn(hh, jnp.where(keep, q, jnp.zeros_like(q))))
        return jnp.where(upper, outs[1], outs[0])

    def ctx_block(i, carry):
        rows = pl.ds(pl.multiple_of(i * w, w), w)

        def attend(hh, qh):
            sc = _dot_nt(qh, kc)
            p = jnp.exp(sc - jnp.max(sc, axis=-1, keepdims=True))
            return _dot(p.astype(BF16), vc) / jnp.sum(p, axis=-1, keepdims=True)

        o_ref[0, rows, :] = heads(q_ref[0, rows, :], attend).astype(BF16)
        return carry

    lax.fori_loop(0, n_ctx // w, ctx_block, 0)

    nr = NAT_ROWS

    def row_group(gi, carry):
        qrows = pl.ds(pl.multiple_of(n_ctx + gi * (nr * w), nr * w), nr * w)
        qg = q_ref[0, qrows, :]
        lane_g = lax.broadcasted_iota(jnp.int32, qg.shape, 1) >= HEAD_DIM
        qh = (jnp.where(lane_g, jnp.zeros_like(qg), qg), jnp.where(lane_g, qg, jnp.zeros_like(qg)))
        s_ctx = [_dot_nt(qh[hh], kc) for hh in range(2)]
        s_win = {}
        vbs = []
        for rr in range(nr):
            r = gi * nr + rr
            r0 = jnp.clip(r - WIN_R // 2, 0, grid_rows - WIN_R)
            cls = r - r0
            krows = pl.ds(pl.multiple_of(n_ctx + r0 * w, w), band)
            kb = k_ref[0, krows, :]
            vbs.append(v_ref[0, krows, :])
            for hh in range(2):
                s_win[rr, hh] = _dot_nt(qh[hh][rr * w:(rr + 1) * w], kb) + bias_ref[hh, cls]
        p_win = {}
        p_ctx = [[], []]
        den = {}
        for rr in range(nr):
            for hh in range(2):
                sw = s_win[rr, hh]
                sc = s_ctx[hh][rr * w:(rr + 1) * w]
                m = jnp.maximum(jnp.max(sw, axis=-1, keepdims=True),
                                jnp.max(sc, axis=-1, keepdims=True))
                pw = jnp.exp(sw - m)
                pc = jnp.exp(sc - m)
                den[rr, hh] = jnp.sum(pw, axis=-1, keepdims=True) + jnp.sum(pc, axis=-1, keepdims=True)
                p_win[rr, hh] = pw.astype(BF16)
                p_ctx[hh].append(pc.astype(BF16))
        o_ctx = [_dot(jnp.concatenate(p_ctx[hh], axis=0), vc) for hh in range(2)]
        o_win = {(rr, hh): _dot(p_win[rr, hh], vbs[rr]) for rr in range(nr) for hh in range(2)}
        for rr in range(nr):
            outs = [(o_win[rr, hh] + o_ctx[hh][rr * w:(rr + 1) * w]) / den[rr, hh] for hh in range(2)]
            rows = pl.ds(pl.multiple_of(n_ctx + (gi * nr + rr) * w, w), w)
            o_ref[0, rows, :] = jnp.where(upper, outs[1], outs[0]).astype(BF16)
        return carry

    lax.fori_loop(0, grid_rows // nr, row_group, 0)


def _natten(cq, ck, cv, bias, n_ctx):
    b, s, w = cq.shape
    grid_rows = (s - n_ctx) // GRID_W
    assert grid_rows >= WIN_R and grid_rows % NAT_ROWS == 0
    blk = lambda: pl.BlockSpec((1, s, LANES), lambda i, p: (i, 0, p))
    return pl.pallas_call(
        functools.partial(_natten_kernel, n_ctx=n_ctx, grid_rows=grid_rows),
        grid=(b, w // LANES),
        in_specs=[blk(), blk(), blk(),
                  pl.BlockSpec((2, WIN_R, GRID_W, WIN_R * GRID_W), lambda i, p: (p, 0, 0, 0))],
        out_specs=blk(),
        out_shape=jax.ShapeDtypeStruct((b, s, w), BF16),
        compiler_params=_params("parallel", "parallel"),
        name="neighborhood_attention",
    )(cq, ck, cv, bias)


def _merge_kernel(x_ref, oaf_ref, oab_ref, og_ref, yb_ref, yc_ref, gt_ref, mod_ref, gn_ref, wbr_ref, wo_ref,
                  o_ref, *, d):
    ya = []
    for h in range(A_HEADS):
        sl = slice(h * A_DK, (h + 1) * A_DK)
        oh = oaf_ref[0, :, sl] + oab_ref[0, :, sl]
        yh = oh * lax.rsqrt(jnp.mean(oh * oh, axis=-1, keepdims=True) + EPS) * gn_ref[...]
        ya.append((yh * _silu(og_ref[0, :, sl].astype(F32))).astype(BF16))
    ys = (jnp.concatenate(ya, axis=-1), yb_ref[0], yc_ref[0])
    merged = None
    for k in range(N_BRANCH):
        term = gt_ref[0, :, k * d:(k + 1) * d].astype(F32) * _dot(ys[k], wbr_ref[k])
        merged = term if merged is None else merged + term
    g1 = mod_ref[0, 0, :, 2 * d:3 * d]
    o_ref[0] = x_ref[0] + g1 * _dot(merged.astype(BF16), wo_ref[...])


def _merge(xs, oaf, oab, og, yb, yc, gt, modsel, gn, wbr, wo, n_ctx_tiles, tile0, n_tiles):
    b, _, d = xs.shape
    tm = TOKEN_TILE
    tok = lambda w: pl.BlockSpec((1, tm, w), lambda i, j: (i, j + tile0, 0))
    return pl.pallas_call(
        functools.partial(_merge_kernel, d=d),
        grid=(b, n_tiles),
        in_specs=[
            tok(d), tok(oaf.shape[-1]), tok(oab.shape[-1]), tok(og.shape[-1]), tok(yb.shape[-1]), tok(yc.shape[-1]),
            tok(gt.shape[-1]),
            pl.BlockSpec((1, 1, 1, modsel.shape[-1]),
                         lambda i, j: (i, (j + tile0 >= n_ctx_tiles).astype(jnp.int32), 0, 0)),
            _resident((1, A_DK), lambda i, j: (0, 0)),
            _resident(wbr.shape, lambda i, j: (0, 0, 0)),
            _resident(wo.shape, lambda i, j: (0, 0)),
        ],
        out_specs=pl.BlockSpec((1, tm, d), lambda i, j: (i, j, 0)),
        out_shape=jax.ShapeDtypeStruct((b, n_tiles * tm, d), F32),
        compiler_params=_params("parallel", "parallel"),
        name="merge_out_projection",
    )(xs, oaf, oab, og, yb, yc, gt, modsel, gn, wbr, wo)


def _ffn_kernel(x_ref, mod_ref, nw_ref, wgu_ref, wd_ref, o_ref, *, d, hidden, n_split):
    x = x_ref[0]
    y = x * lax.rsqrt(jnp.mean(x * x, axis=-1, keepdims=True) + EPS) * nw_ref[...]
    sh2 = mod_ref[0, 0, :, 3 * d:4 * d]
    sc2 = mod_ref[0, 0, :, 4 * d:5 * d]
    g2 = mod_ref[0, 0, :, 5 * d:6 * d]
    h = (y * (1.0 + sc2) + sh2).astype(BF16)
    step = hidden // n_split
    acc = None
    for c in range(n_split):
        a = _dot(h, wgu_ref[:, c * step:(c + 1) * step])
        g = _dot(h, wgu_ref[:, hidden + c * step:hidden + (c + 1) * step])
        part = _dot((_silu(a) * g).astype(BF16), wd_ref[c * step:(c + 1) * step, :])
        acc = part if acc is None else acc + part
    o_ref[0] = x + g2 * acc


def _ffn(xs, modsel, norm_w, wgu, wd, n_ctx_tiles, tile0):
    b, s, d = xs.shape
    tm = TOKEN_TILE
    hidden = wd.shape[0]
    n_split = 2
    assert hidden % (n_split * LANES) == 0
    return pl.pallas_call(
        functools.partial(_ffn_kernel, d=d, hidden=hidden, n_split=n_split),
        grid=(b, s // tm),
        in_specs=[
            pl.BlockSpec((1, tm, d), lambda i, j: (i, j, 0)),
            pl.BlockSpec((1, 1, 1, modsel.shape[-1]),
                         lambda i, j: (i, (j + tile0 >= n_ctx_tiles).astype(jnp.int32), 0, 0)),
            _resident((1, d), lambda i, j: (0, 0)),
            _resident(wgu.shape, lambda i, j: (0, 0)),
            _resident(wd.shape, lambda i, j: (0, 0)),
        ],
        out_specs=pl.BlockSpec((1, tm, d), lambda i, j: (i, j, 0)),
        out_shape=jax.ShapeDtypeStruct((b, s, d), F32),
        compiler_params=_params("parallel", "parallel"),
        name="swiglu_ffn",
    )(xs, modsel, norm_w, wgu, wd)


def _rope_tables(t, n_ctx):
    pos = jnp.arange(t)
    row = (pos // GRID_W).astype(F32)
    colp = (pos % GRID_W).astype(F32)
    n = HEAD_DIM // 4
    inv = ROPE_THETA ** (-jnp.arange(n, dtype=F32) / n)
    ang = jnp.concatenate([row[:, None] * inv, colp[:, None] * inv], axis=-1)
    cos = jnp.cos(ang)
    sin = jnp.sin(ang)
    reps = LANES // HEAD_DIM
    cos_t = jnp.tile(jnp.concatenate([cos, cos], axis=-1), (1, reps))
    sin_t = jnp.tile(jnp.concatenate([-sin, sin], axis=-1), (1, reps))
    cos_t = jnp.concatenate([jnp.ones((n_ctx, LANES), F32), cos_t], axis=0)
    sin_t = jnp.concatenate([jnp.zeros((n_ctx, LANES), F32), sin_t], axis=0)
    return cos_t, sin_t


def _deinterleave_heads(a):
    lead = a.shape[:-1]
    n = a.shape[-1] // HEAD_DIM
    a = a.reshape(lead + (n, HEAD_DIM // 2, 2))
    return jnp.swapaxes(a, -1, -2).reshape(lead + (n * HEAD_DIM,))


def _inproj_weight(w):
    b0 = 5 * A_HEADS * A_DK
    b1 = b0 + (B_HEADS + B_KV_HEADS) * HEAD_DIM
    return jnp.concatenate([w[:, :b0], _deinterleave_heads(w[:, b0:b1]), w[:, b1:]],
                           axis=1).astype(BF16)


def kernel(x, c, ctx, c_ctx, w_mod, b_mod, norm_mix, norm_ffn, w_in, lb_raw, gn_a, qn_b, kn_b,
           qn_c, kn_c, rel_bias, w_branch, w_out, w_gate_up, w_down):
    bn, t, d = x.shape
    n_ctx = ctx.shape[1]
    depth = w_mod.shape[0]
    tm = TOKEN_TILE
    assert n_ctx % tm == 0 and t % tm == 0 and t % GRID_W == 0
    n_ctx_tiles = n_ctx // tm
    aw = A_HEADS * A_DK

    c_all = jnp.zeros((8, d), F32).at[:bn].set(c).at[bn].set(c_ctx)
    mod = _modulation(c_all, w_mod, b_mod)
    modsel = jnp.stack([jnp.broadcast_to(mod[:, bn:bn + 1], (depth, bn, 6 * d)), mod[:, :bn]],
                       axis=2)[:, :, :, None, :]

    lbp = jax.nn.softmax(lb_raw.astype(F32), axis=0)
    lb_all = jnp.clip(jnp.cumsum(lbp, axis=0) - lbp[:1], 0.0, 1.0 - 1e-6)
    zeros2 = jnp.zeros_like(lb_all)
    lbc_all = jnp.concatenate([jnp.log(lb_all + LB_TINY), jnp.log1p(-lb_all), 1.0 - lb_all, zeros2],
                              axis=1)

    cos_t, sin_t = _rope_tables(t, n_ctx)
    lane = np.arange(LANES)
    gmat = jnp.asarray((lane[:, None] // HEAD_DIM == lane[None, :] // HEAD_DIM) / HEAD_DIM, BF16)
    dmat, smask, qrole = _hgrn_tables()
    tables = (jnp.asarray(dmat, BF16), jnp.asarray(smask), jnp.asarray(qrole))

    xs = jnp.concatenate([ctx, x], axis=1)
    for l in range(depth):
        last = l == depth - 1
        w_bf = _inproj_weight(w_in[l])
        headw = jnp.stack([jnp.tile(_deinterleave_heads(qn_b[l]), B_HEADS),
                           jnp.tile(_deinterleave_heads(kn_b[l]), B_HEADS),
                           jnp.tile(qn_c[l], C_HEADS), jnp.tile(kn_c[l], C_HEADS)]).astype(F32)
        headw = jnp.concatenate([headw, jnp.zeros((4, headw.shape[1]), F32)], axis=0)
        (aq, alff, akf, alfb, akb, av, aog, bq, bk, bv, cq, ck, cv, gt) = _in_projection(
            xs, modsel[l], norm_mix[l][None, :], w_bf, lbc_all[l], headw, cos_t, sin_t, gmat,
            n_ctx_tiles)
        oaf, oab = _hgrn(aq, alff, akf, alfb, akb, av, tables, n_ctx)
        yb = _gqa(bq, bk, bv, n_ctx)
        yc = _natten(cq, ck, cv, _natten_bias(rel_bias[l]), n_ctx)
        tile0 = n_ctx_tiles if last else 0
        n_tiles = (t if last else t + n_ctx) // tm
        xm = _merge(xs, oaf, oab, aog, yb, yc, gt, modsel[l], gn_a[l][None, :],
                    w_branch[l].astype(BF16), w_out[l].astype(BF16), n_ctx_tiles, tile0, n_tiles)
        xs = _ffn(xm, modsel[l], norm_ffn[l][None, :], w_gate_up[l].astype(BF16),
                  w_down[l].astype(BF16), n_ctx_tiles, tile0)
    return xs
```

```python
import functools

import numpy as np
import jax
import jax.numpy as jnp
from jax import lax
from jax.experimental import pallas as pl
from jax.experimental.pallas import tpu as pltpu

GRID_W = 64
A_HEADS = 4
A_DK = 128
B_HEADS = 8
B_KV_HEADS = 2
B_GROUP = B_HEADS // B_KV_HEADS
HEAD_DIM = 64
C_HEADS = 8
WIN_R = 8
WIN_C = 16
N_BRANCH = 3
ROPE_THETA = 10000.0
EPS = 1e-6
LB_TINY = 1e-30

LANES = 128
TOKEN_TILE = 256
A_CHUNK = 64
A_LEVELS = 6
KV_TILE = 512
Q_TILE = 256
NAT_ROWS = 4
NEG_BIG = -1e30
LOG2_E = 1.4426950408889634
BOUND_SLACK = 1.02
MAX_SAFE_SHIFT = 50.0
VMEM_LIMIT = 56 * 1024 * 1024

F32 = jnp.float32
BF16 = jnp.bfloat16


def _dot(a, b):
    return jnp.dot(a, b, preferred_element_type=F32)


def _dot_nt(a, b):
    return lax.dot_general(a, b, (((1,), (1,)), ((), ())), preferred_element_type=F32)


def _dot_tn(a, b):
    return lax.dot_general(a, b, (((0,), (0,)), ((), ())), preferred_element_type=F32)


def _split_bf16(x):
    hi = x.astype(BF16)
    lo = (x - hi.astype(F32)).astype(BF16)
    return hi, lo


def _silu(x):
    return x * jax.nn.sigmoid(x)


def _params(*sem):
    return pltpu.CompilerParams(dimension_semantics=sem, vmem_limit_bytes=VMEM_LIMIT)


def _resident(shape, index_map):
    return pl.BlockSpec(shape, index_map, pipeline_mode=pl.Buffered(1))


def _mod_kernel(c_ref, w_ref, b_ref, o_ref):
    a = _silu(c_ref[...])
    a_hi, a_lo = _split_bf16(a)
    w = w_ref[0]
    w_hi, w_lo = _split_bf16(w)
    acc = _dot(a_hi, w_hi) + _dot(a_hi, w_lo) + _dot(a_lo, w_hi)
    o_ref[0] = acc + b_ref[0]


def _modulation(c_all, w_mod, b_mod):
    depth, d, n = w_mod.shape
    rows = c_all.shape[0]
    tn = 1536
    return pl.pallas_call(
        _mod_kernel,
        grid=(depth, n // tn),
        in_specs=[
            pl.BlockSpec((rows, d), lambda l, j: (0, 0)),
            pl.BlockSpec((1, d, tn), lambda l, j: (l, 0, j)),
            pl.BlockSpec((1, 1, tn), lambda l, j: (l, 0, j)),
        ],
        out_specs=pl.BlockSpec((1, rows, tn), lambda l, j: (l, 0, j)),
        out_shape=jax.ShapeDtypeStruct((depth, rows, n), F32),
        compiler_params=_params("arbitrary", "arbitrary"),
        name="modulation",
    )(c_all, w_mod, b_mod.reshape(depth, 1, n))


def _group_mean_sq(x, gmat):
    hi, lo = _split_bf16(x * x)
    return _dot(hi, gmat) + _dot(lo, gmat)


def _head_norm(x, gmat, w):
    return x * lax.rsqrt(_group_mean_sq(x, gmat) + EPS) * w


def _rope(x, cos, sin_signed, first_half):
    rot = jnp.where(first_half, pltpu.roll(x, LANES - HEAD_DIM // 2, 1),
                    pltpu.roll(x, HEAD_DIM // 2, 1))
    return x * cos + rot * sin_signed


def _log_forget(z, la, l1):
    ls = jnp.minimum(z, 0.0) - jnp.log1p(jnp.exp(-jnp.abs(z)))
    t = l1 + ls
    return jnp.maximum(la, t) + jnp.log1p(jnp.exp(-jnp.abs(la - t)))


def _inproj_kernel(x_ref, mod_ref, nw_ref, w_ref, lbc_ref, hw_ref, cos_ref, sin_ref, gmat_ref,
                   aq_ref, alff_ref, akf_ref, alfb_ref, akb_ref, av_ref, aog_ref,
                   bq_ref, bk_ref, bv_ref, cq_ref, ck_ref, cv_ref, gt_ref, *, d):
    x = x_ref[0]
    ms = jnp.mean(x * x, axis=-1, keepdims=True)
    y = x * lax.rsqrt(ms + EPS) * nw_ref[...]
    sh1 = mod_ref[0, 0, :, 0:d]
    sc1 = mod_ref[0, 0, :, d:2 * d]
    u = (y * (1.0 + sc1) + sh1).astype(BF16)

    aw = A_HEADS * A_DK
    col = [0]

    def proj(width):
        c0 = col[0]
        col[0] = c0 + width
        return _dot(u, w_ref[:, c0:c0 + width])

    aq_ref[0] = (proj(aw) * (A_DK ** -0.5)).astype(BF16)
    for lf_ref, k_ref, row in ((alff_ref, akf_ref, 0), (alfb_ref, akb_ref, 1)):
        z = proj(aw)
        la = lbc_ref[row:row + 1, :]
        l1 = lbc_ref[2 + row:3 + row, :]
        oml = lbc_ref[4 + row:5 + row, :]
        lf_ref[0] = _log_forget(z, la, l1)
        k_ref[0] = (oml * jax.nn.sigmoid(-z)).astype(BF16)
    av_ref[0] = proj(aw).astype(BF16)
    aog_ref[0] = proj(aw).astype(BF16)

    gmat = gmat_ref[...]
    lane = lax.broadcasted_iota(jnp.int32, (x.shape[0], LANES), 1)
    first_half = (lane % HEAD_DIM) < (HEAD_DIM // 2)
    cos = cos_ref[...]
    sin = sin_ref[...]
    scale = HEAD_DIM ** -0.5

    zq = proj(B_HEADS * HEAD_DIM)
    for j in range(B_HEADS * HEAD_DIM // LANES):
        sl = slice(j * LANES, (j + 1) * LANES)
        qn = _head_norm(zq[:, sl], gmat, hw_ref[0:1, sl])
        bq_ref[0, :, sl] = (_rope(qn, cos, sin, first_half) * (scale * LOG2_E)).astype(BF16)
    zk = proj(B_KV_HEADS * HEAD_DIM)
    kn = _head_norm(zk, gmat, hw_ref[1:2, 0:LANES])
    bk_ref[0] = _rope(kn, cos, sin, first_half).astype(BF16)
    bv_ref[0] = proj(B_KV_HEADS * HEAD_DIM).astype(BF16)

    zq = proj(C_HEADS * HEAD_DIM)
    for j in range(C_HEADS * HEAD_DIM // LANES):
        sl = slice(j * LANES, (j + 1) * LANES)
        cq_ref[0, :, sl] = (_head_norm(zq[:, sl], gmat, hw_ref[2:3, sl]) * scale).astype(BF16)
    zk = proj(C_HEADS * HEAD_DIM)
    for j in range(C_HEADS * HEAD_DIM // LANES):
        sl = slice(j * LANES, (j + 1) * LANES)
        ck_ref[0, :, sl] = _head_norm(zk[:, sl], gmat, hw_ref[3:4, sl]).astype(BF16)
    cv_ref[0] = proj(C_HEADS * HEAD_DIM).astype(BF16)

    for k in range(N_BRANCH):
        gt_ref[0, :, k * d:(k + 1) * d] = jax.nn.sigmoid(proj(d)).astype(BF16)


def _in_projection(xs, modsel, norm_w, w_bf, lbc, headw, cos_t, sin_t, gmat, n_ctx_tiles):
    b, s, d = xs.shape
    tm = TOKEN_TILE
    aw = A_HEADS * A_DK
    n_in = w_bf.shape[1]
    widths = [(aw, BF16), (aw, F32), (aw, BF16), (aw, F32), (aw, BF16), (aw, BF16), (aw, BF16),
              (B_HEADS * HEAD_DIM, BF16), (B_KV_HEADS * HEAD_DIM, BF16), (B_KV_HEADS * HEAD_DIM, BF16),
              (C_HEADS * HEAD_DIM, BF16), (C_HEADS * HEAD_DIM, BF16), (C_HEADS * HEAD_DIM, BF16),
              (N_BRANCH * d, BF16)]
    tok = lambda w: pl.BlockSpec((1, tm, w), lambda i, j: (i, j, 0))
    return pl.pallas_call(
        functools.partial(_inproj_kernel, d=d),
        grid=(b, s // tm),
        in_specs=[
            tok(d),
            pl.BlockSpec((1, 1, 1, modsel.shape[-1]),
                         lambda i, j: (i, (j >= n_ctx_tiles).astype(jnp.int32), 0, 0)),
            _resident((1, d), lambda i, j: (0, 0)),
            _resident((d, n_in), lambda i, j: (0, 0)),
            _resident(lbc.shape, lambda i, j: (0, 0)),
            _resident(headw.shape, lambda i, j: (0, 0)),
            pl.BlockSpec((tm, LANES), lambda i, j: (j, 0)),
            pl.BlockSpec((tm, LANES), lambda i, j: (j, 0)),
            _resident((LANES, LANES), lambda i, j: (0, 0)),
        ],
        out_specs=[tok(w) for w, _ in widths],
        out_shape=[jax.ShapeDtypeStruct((b, s, w), dt) for w, dt in widths],
        compiler_params=_params("parallel", "parallel"),
        name="in_projection",
    )(xs, modsel, norm_w, w_bf, lbc, headw, cos_t, sin_t, gmat)


def _hgrn_tables():
    L = A_CHUNK
    ms = [1 << i for i in range(A_LEVELS)]
    dmat = np.zeros((2, (2 + A_LEVELS) * L, L), np.float32)
    smask = np.zeros((2, A_LEVELS + 1, L, L), np.float32)
    qrole = np.zeros((2, A_LEVELS, L, LANES), np.float32)
    t = np.arange(L)
    for dr in range(2):
        fwd = dr == 0
        for r in range(L):
            if fwd:
                dmat[dr, r, :r + 1] = 1.0
                dmat[dr, L + r, r + 1:] = 1.0
            else:
                dmat[dr, r, r:] = 1.0
                dmat[dr, L + r, :r] = 1.0
        for li, m in enumerate(ms):
            base = (2 + li) * L
            for r in range(L):
                start = (r // (2 * m)) * 2 * m
                mid = start + m
                upper = r >= mid
                if fwd:
                    if upper:
                        dmat[dr, base + r, mid:r + 1] = 1.0
                    else:
                        dmat[dr, base + r, r + 1:mid] = 1.0
                else:
                    if not upper:
                        dmat[dr, base + r, r:mid] = 1.0
                    else:
                        dmat[dr, base + r, mid:r] = 1.0
                qrole[dr, li, r, :] = 1.0 if (upper == fwd) else 0.0
            same = (t[:, None] // (2 * m)) == (t[None, :] // (2 * m))
            t_up = (t[:, None] % (2 * m)) >= m
            s_up = (t[None, :] % (2 * m)) >= m
            if fwd:
                smask[dr, li] = same & t_up & ~s_up
            else:
                smask[dr, li] = same & ~t_up & s_up
        smask[dr, A_LEVELS] = np.eye(L)
    return dmat, smask, qrole


def _hgrn_kernel(qf_ref, lff_ref, kf_ref, vf_ref, qb_ref, lfb_ref, kb_ref, vb_ref,
                 dmat_ref, smask_ref, qrole_ref, of_ref, ob_ref, stf_ref, stb_ref, *, n_chunks):
    L = A_CHUNK

    @pl.when(pl.program_id(1) == 0)
    def _():
        stf_ref[...] = jnp.zeros_like(stf_ref)
        stb_ref[...] = jnp.zeros_like(stb_ref)

    dirs = ((0, qf_ref, lff_ref, kf_ref, vf_ref, of_ref, stf_ref),
            (1, qb_ref, lfb_ref, kb_ref, vb_ref, ob_ref, stb_ref))

    def body(i, carry):
        chains = []
        for dr, q_ref, lf_ref, k_ref, v_ref, o_ref, st_ref in dirs:
            c = i if dr == 0 else n_chunks - 1 - i
            rows = pl.ds(pl.multiple_of(c * L, L), L)
            dm = dmat_ref[dr]
            for hp in range(A_HEADS // 2):
                g_hi, g_lo = _split_bf16(lf_ref[0, rows, 2 * hp * A_DK:(2 * hp + 2) * A_DK])
                ex2 = jnp.exp(_dot(dm, g_hi) + _dot(dm, g_lo))
                for hh in range(2):
                    h = 2 * hp + hh
                    chains.append((dr, h, rows, ex2[:, hh * A_DK:(hh + 1) * A_DK],
                                   q_ref, k_ref, v_ref, o_ref, st_ref))
        staged = []
        for dr, h, rows, ex, q_ref, k_ref, v_ref, o_ref, st_ref in chains:
            cols = slice(h * A_DK, (h + 1) * A_DK)
            q = q_ref[0, rows, cols].astype(F32)
            k = k_ref[0, rows, cols].astype(F32)
            vb = v_ref[0, rows, cols]
            eb = ex[0:L]
            ebl = eb[L - 1:L] if dr == 0 else eb[0:1]
            st = st_ref[h]
            o = _dot_nt((q * eb).astype(BF16), st.astype(BF16))
            st_ref[h] = st * ebl + _dot_tn(vb, (k * ex[L:2 * L]).astype(BF16))
            parts = [jnp.sum(q * k, axis=-1, keepdims=True)]
            for li in range(A_LEVELS):
                role = qrole_ref[dr, li] > 0.5
                xl = (jnp.where(role, q, k) * ex[(2 + li) * L:(3 + li) * L]).astype(BF16)
                parts.append(_dot_nt(xl, xl))
            staged.append((dr, rows, cols, o, parts, vb, o_ref))
        for dr, rows, cols, o, parts, vb, o_ref in staged:
            scores = parts[0] * smask_ref[dr, A_LEVELS]
            for li in range(A_LEVELS):
                scores = scores + parts[1 + li] * smask_ref[dr, li]
            o_ref[0, rows, cols] = o + _dot(scores.astype(BF16), vb)
        return carry

    lax.fori_loop(0, n_chunks, body, 0)


def _hgrn(aq, alff, akf, alfb, akb, av, tables, n_ctx):
    b, s, w = aq.shape
    ts = TOKEN_TILE
    n_tiles = s // ts
    n_ctx_tiles = n_ctx // ts
    dmat, smask, qrole = tables

    def bwd_tile(j):
        return jnp.where(j < n_ctx_tiles, n_ctx_tiles - 1 - j, n_tiles - 1 - (j - n_ctx_tiles))

    fwd = lambda: pl.BlockSpec((1, ts, w), lambda i, j: (i, j, 0))
    bwd = lambda: pl.BlockSpec((1, ts, w), lambda i, j: (i, bwd_tile(j), 0))
    const = lambda a: _resident(a.shape, lambda i, j: (0,) * a.ndim)
    state = pltpu.VMEM((A_HEADS, A_DK, A_DK), F32)
    return pl.pallas_call(
        functools.partial(_hgrn_kernel, n_chunks=ts // A_CHUNK),
        grid=(b, n_tiles),
        in_specs=[fwd(), fwd(), fwd(), fwd(), bwd(), bwd(), bwd(), bwd(),
                  const(dmat), const(smask), const(qrole)],
        out_specs=[fwd(), bwd()],
        out_shape=[jax.ShapeDtypeStruct((b, s, w), F32)] * 2,
        scratch_shapes=[state, state],
        compiler_params=_params("parallel", "arbitrary"),
        name="hgrn2_scan",
    )(aq, alff, akf, av, aq, alfb, akb, av, dmat, smask, qrole)


def _gqa_kernel(q_ref, k_ref, v_ref, o_ref, qs_ref, kmax_ref, *, n_ctx, s):
    tq = Q_TILE
    tk = KV_TILE
    j = pl.program_id(1)
    is_latent = j >= n_ctx // tq
    n_lat = (s - n_ctx) // tk
    lane = lax.broadcasted_iota(jnp.int32, (tq, LANES), 1)
    upper = lane >= HEAD_DIM
    ctx_rows = pl.ds(0, n_ctx)

    @pl.when(j == 0)
    def _():
        li = lax.broadcasted_iota(jnp.int32, (LANES, LANES), 0) // HEAD_DIM
        lj = lax.broadcasted_iota(jnp.int32, (LANES, LANES), 1) // HEAD_DIM
        head_sum = jnp.where(li == lj, 1.0, 0.0).astype(BF16)

        def body(c, mx):
            kk = k_ref[0, pl.ds(pl.multiple_of(c * tk, tk), tk), :].astype(F32)
            hi, lo = _split_bf16(kk * kk)
            return jnp.maximum(mx, _dot(hi, head_sum) + _dot(lo, head_sum))

        mx = lax.fori_loop(0, n_lat, body, jnp.zeros((tk, LANES), F32))
        kk = k_ref[0, pl.ds(s - n_ctx, n_ctx), :].astype(F32)
        hi, lo = _split_bf16(kk * kk)
        tail = _dot(hi, head_sum) + _dot(lo, head_sum)
        kmax_ref[...] = jnp.maximum(jnp.max(mx, axis=0, keepdims=True),
                                    jnp.max(tail, axis=0, keepdims=True))

    def lat_rows(c):
        return pl.ds(n_ctx + c * tk, tk)

    def scores(rows):
        return _dot_nt(qs_ref[...], k_ref[0, rows, :])

    def lane_max(sc):
        out = sc[:, 0:LANES]
        for jb in range(1, sc.shape[1] // LANES):
            out = jnp.maximum(out, sc[:, jb * LANES:(jb + 1) * LANES])
        return out

    def row_max(lane_wise):
        return jnp.broadcast_to(jnp.max(lane_wise, axis=-1, keepdims=True), lane_wise.shape)

    def probs(sc, mb):
        ps = []
        ls = None
        for jb in range(sc.shape[1] // LANES):
            pj = jnp.exp2(sc[:, jb * LANES:(jb + 1) * LANES] - mb)
            ls = pj if ls is None else ls + pj
            ps.append(pj.astype(BF16))
        return jnp.concatenate(ps, axis=-1), ls

    def finish(kv, ls, acc):
        out = acc * (1.0 / jnp.sum(ls, axis=-1, keepdims=True))
        pair_out = [None, None]
        for gi in range(B_GROUP):
            h = kv * B_GROUP + gi
            oh = out[gi * tq:(gi + 1) * tq, :]
            if h % 2 != kv:
                oh = pltpu.roll(oh, HEAD_DIM, 1)
            pair_out[h % 2] = oh
            if h % 2 == 1:
                o_ref[0, :, (h // 2) * LANES:(h // 2 + 1) * LANES] = jnp.where(
                    upper, pair_out[1], pair_out[0]).astype(BF16)

    def context_only(kv):
        sc = scores(ctx_rows)
        p, ls = probs(sc, row_max(lane_max(sc)))
        finish(kv, ls, _dot(p, v_ref[0, ctx_rows, :]))

    def exact_row_max():
        def body(c, mx):
            rows = pl.ds(pl.multiple_of(n_ctx + c * tk, LANES), tk)
            return jnp.maximum(mx, lane_max(scores(rows)))

        return row_max(lax.fori_loop(0, n_lat, body, lane_max(scores(ctx_rows))))

    def all_keys(kv, shift_fn):
        mb = shift_fn()
        sc_next = scores(ctx_rows)
        ls = None
        acc = None
        for c in range(-1, n_lat):
            rows = ctx_rows if c < 0 else lat_rows(c)
            sc = sc_next
            if c + 1 < n_lat:
                sc_next = scores(lat_rows(c + 1))
            p, lt = probs(sc, mb)
            pv = _dot(p, v_ref[0, rows, :])
            ls = lt if ls is None else ls + lt
            acc = pv if acc is None else acc + pv
        finish(kv, ls, acc)

    for kv in range(B_KV_HEADS):
        for gi in range(B_GROUP):
            h = kv * B_GROUP + gi
            blk = q_ref[0, :, (h // 2) * LANES:(h // 2 + 1) * LANES].astype(F32)
            if h % 2 != kv:
                blk = pltpu.roll(blk, HEAD_DIM, 1)
            keep = upper if kv == 1 else jnp.logical_not(upper)
            qs_ref[gi * tq:(gi + 1) * tq, :] = jnp.where(keep, blk, 0.0).astype(BF16)
        qsq = qs_ref[...].astype(F32)
        qn2 = jnp.sum(qsq * qsq, axis=-1, keepdims=True)
        kn2 = jnp.max(kmax_ref[:, kv * HEAD_DIM:(kv + 1) * HEAD_DIM], axis=-1, keepdims=True)
        bound = jnp.broadcast_to(jnp.sqrt(qn2 * kn2) * BOUND_SLACK, (B_GROUP * tq, LANES))
        bound_ok = jnp.max(bound) <= MAX_SAFE_SHIFT
        pl.when(jnp.logical_and(is_latent, bound_ok))(
            functools.partial(all_keys, kv, lambda: bound))
        pl.when(jnp.logical_and(is_latent, jnp.logical_not(bound_ok)))(
            functools.partial(all_keys, kv, exact_row_max))
        pl.when(jnp.logical_not(is_latent))(functools.partial(context_only, kv))


def _gqa(bq, bk, bv, n_ctx):
    b, s, w = bq.shape
    tq = Q_TILE
    return pl.pallas_call(
        functools.partial(_gqa_kernel, n_ctx=n_ctx, s=s),
        grid=(b, s // tq),
        in_specs=[
            pl.BlockSpec((1, tq, w), lambda i, j: (i, j, 0)),
            pl.BlockSpec((1, s, LANES), lambda i, j: (i, 0, 0)),
            pl.BlockSpec((1, s, LANES), lambda i, j: (i, 0, 0)),
        ],
        out_specs=pl.BlockSpec((1, tq, w), lambda i, j: (i, j, 0)),
        out_shape=jax.ShapeDtypeStruct((b, s, w), BF16),
        scratch_shapes=[pltpu.VMEM((B_GROUP * tq, LANES), BF16), pltpu.VMEM((1, LANES), F32)],
        compiler_params=_params("parallel", "arbitrary"),
        name="gqa_attention",
    )(bq, bk, bv)


def _natten_bias(rel_bias):
    h = rel_bias.shape[0]
    qcol = np.arange(GRID_W)
    c0 = np.clip(qcol - WIN_C // 2, 0, GRID_W - WIN_C)
    kcol = np.arange(GRID_W)
    valid = (kcol[None, :] >= c0[:, None]) & (kcol[None, :] < c0[:, None] + WIN_C)
    dc = kcol[None, :] - qcol[:, None] + (WIN_C - 1)
    pick = (dc[None, :, :] == np.arange(2 * WIN_C - 1)[:, None, None]) & valid[None]
    cols = jnp.einsum('hdi,iqk->hdqk', rel_bias.astype(F32), jnp.asarray(pick, F32),
                      precision=lax.Precision.HIGHEST)
    cols = jnp.where(valid[None, None], cols, NEG_BIG)
    per_class = [jnp.swapaxes(cols[:, WIN_R - 1 - c:2 * WIN_R - 1 - c], 1, 2) for c in range(WIN_R)]
    return jnp.stack(per_class, axis=1).reshape(h, WIN_R, GRID_W, WIN_R * GRID_W)


def _natten_kernel(q_ref, k_ref, v_ref, bias_ref, o_ref, *, n_ctx, grid_rows):
    w = GRID_W
    band = WIN_R * w
    lane = lax.broadcasted_iota(jnp.int32, (w, LANES), 1)
    upper = lane >= HEAD_DIM
    kc = k_ref[0, 0:n_ctx, :]
    vc = v_ref[0, 0:n_ctx, :]

    def heads(q, fn):
        outs = []
        for hh in range(2):
            keep = upper if hh == 1 else jnp.logical_not(upper)
            outs.append(fn(hh, jnp.where(keep, q, jnp.zeros_like(q))))
        return jnp.where(upper, outs[1], outs[0])

    def ctx_block(i, carry):
        rows = pl.ds(pl.multiple_of(i * w, w), w)

        def attend(hh, qh):
            sc = _dot_nt(qh, kc)
            p = jnp.exp(sc - jnp.max(sc, axis=-1, keepdims=True))
            return _dot(p.astype(BF16), vc) / jnp.sum(p, axis=-1, keepdims=True)

        o_ref[0, rows, :] = heads(q_ref[0, rows, :], attend).astype(BF16)
        return carry

    lax.fori_loop(0, n_ctx // w, ctx_block, 0)

    nr = NAT_ROWS

    def row_group(gi, carry):
        qrows = pl.ds(pl.multiple_of(n_ctx + gi * (nr * w), nr * w), nr * w)
        qg = q_ref[0, qrows, :]
        lane_g = lax.broadcasted_iota(jnp.int32, qg.shape, 1) >= HEAD_DIM
        qh = (jnp.where(lane_g, jnp.zeros_like(qg), qg), jnp.where(lane_g, qg, jnp.zeros_like(qg)))
        s_ctx = [_dot_nt(qh[hh], kc) for hh in range(2)]
        s_win = {}
        vbs = []
        for rr in range(nr):
            r = gi * nr + rr
            r0 = jnp.clip(r - WIN_R // 2, 0, grid_rows - WIN_R)
            cls = r - r0
            krows = pl.ds(pl.multiple_of(n_ctx + r0 * w, w), band)
            kb = k_ref[0, krows, :]
            vbs.append(v_ref[0, krows, :])
            for hh in range(2):
                s_win[rr, hh] = _dot_nt(qh[hh][rr * w:(rr + 1) * w], kb) + bias_ref[hh, cls]
        p_win = {}
        p_ctx = [[], []]
        den = {}
        for rr in range(nr):
            for hh in range(2):
                sw = s_win[rr, hh]
                sc = s_ctx[hh][rr * w:(rr + 1) * w]
                m = jnp.maximum(jnp.max(sw, axis=-1, keepdims=True),
                                jnp.max(sc, axis=-1, keepdims=True))
                pw = jnp.exp(sw - m)
                pc = jnp.exp(sc - m)
                den[rr, hh] = jnp.sum(pw, axis=-1, keepdims=True) + jnp.sum(pc, axis=-1, keepdims=True)
                p_win[rr, hh] = pw.astype(BF16)
                p_ctx[hh].append(pc.astype(BF16))
        o_ctx = [_dot(jnp.concatenate(p_ctx[hh], axis=0), vc) for hh in range(2)]
        o_win = {(rr, hh): _dot(p_win[rr, hh], vbs[rr]) for rr in range(nr) for hh in range(2)}
        for rr in range(nr):
            outs = [(o_win[rr, hh] + o_ctx[hh][rr * w:(rr + 1) * w]) / den[rr, hh] for hh in range(2)]
            rows = pl.ds(pl.multiple_of(n_ctx + (gi * nr + rr) * w, w), w)
            o_ref[0, rows, :] = jnp.where(upper, outs[1], outs[0]).astype(BF16)
        return carry

    lax.fori_loop(0, grid_rows // nr, row_group, 0)


def _natten(cq, ck, cv, bias, n_ctx):
    b, s, w = cq.shape
    grid_rows = (s - n_ctx) // GRID_W
    assert grid_rows >= WIN_R and grid_rows % NAT_ROWS == 0
    blk = lambda: pl.BlockSpec((1, s, LANES), lambda i, p: (i, 0, p))
    return pl.pallas_call(
        functools.partial(_natten_kernel, n_ctx=n_ctx, grid_rows=grid_rows),
        grid=(b, w // LANES),
        in_specs=[blk(), blk(), blk(),
                  pl.BlockSpec((2, WIN_R, GRID_W, WIN_R * GRID_W), lambda i, p: (p, 0, 0, 0))],
        out_specs=blk(),
        out_shape=jax.ShapeDtypeStruct((b, s, w), BF16),
        compiler_params=_params("parallel", "parallel"),
        name="neighborhood_attention",
    )(cq, ck, cv, bias)


def _merge_ffn_kernel(x_ref, oaf_ref, oab_ref, og_ref, yb_ref, yc_ref, gt_ref, mod_ref, gn_ref, wbr_ref,
                      wo_ref, nw_ref, wgu_ref, wd_ref, o_ref, *, d, hidden, n_split):
    ya = []
    for h in range(A_HEADS):
        sl = slice(h * A_DK, (h + 1) * A_DK)
        oh = oaf_ref[0, :, sl] + oab_ref[0, :, sl]
        yh = oh * lax.rsqrt(jnp.mean(oh * oh, axis=-1, keepdims=True) + EPS) * gn_ref[...]
        ya.append((yh * _silu(og_ref[0, :, sl].astype(F32))).astype(BF16))
    ys = (jnp.concatenate(ya, axis=-1), yb_ref[0], yc_ref[0])
    merged = None
    for k in range(N_BRANCH):
        term = gt_ref[0, :, k * d:(k + 1) * d].astype(F32) * _dot(ys[k], wbr_ref[k])
        merged = term if merged is None else merged + term
    g1 = mod_ref[0, 0, :, 2 * d:3 * d]
    x = x_ref[0] + g1 * _dot(merged.astype(BF16), wo_ref[...])

    y = x * lax.rsqrt(jnp.mean(x * x, axis=-1, keepdims=True) + EPS) * nw_ref[...]
    sh2 = mod_ref[0, 0, :, 3 * d:4 * d]
    sc2 = mod_ref[0, 0, :, 4 * d:5 * d]
    g2 = mod_ref[0, 0, :, 5 * d:6 * d]
    h = (y * (1.0 + sc2) + sh2).astype(BF16)
    step = hidden // n_split
    acc = None
    for c in range(n_split):
        a = _dot(h, wgu_ref[:, c * step:(c + 1) * step])
        g = _dot(h, wgu_ref[:, hidden + c * step:hidden + (c + 1) * step])
        part = _dot((_silu(a) * g).astype(BF16), wd_ref[c * step:(c + 1) * step, :])
        acc = part if acc is None else acc + part
    o_ref[0] = x + g2 * acc


def _merge_ffn(xs, oaf, oab, og, yb, yc, gt, modsel, gn, wbr, wo, norm_w, wgu, wd,
               n_ctx_tiles, tile0, n_tiles):
    b, _, d = xs.shape
    tm = TOKEN_TILE
    hidden = wd.shape[0]
    n_split = 2
    assert hidden % (n_split * LANES) == 0
    tok = lambda w: pl.BlockSpec((1, tm, w), lambda i, j: (i, j + tile0, 0))
    return pl.pallas_call(
        functools.partial(_merge_ffn_kernel, d=d, hidden=hidden, n_split=n_split),
        grid=(b, n_tiles),
        in_specs=[
            tok(d), tok(oaf.shape[-1]), tok(oab.shape[-1]), tok(og.shape[-1]), tok(yb.shape[-1]), tok(yc.shape[-1]),
            tok(gt.shape[-1]),
            pl.BlockSpec((1, 1, 1, modsel.shape[-1]),
                         lambda i, j: (i, (j + tile0 >= n_ctx_tiles).astype(jnp.int32), 0, 0)),
            _resident((1, A_DK), lambda i, j: (0, 0)),
            _resident(wbr.shape, lambda i, j: (0, 0, 0)),
            _resident(wo.shape, lambda i, j: (0, 0)),
            _resident((1, d), lambda i, j: (0, 0)),
            _resident(wgu.shape, lambda i, j: (0, 0)),
            _resident(wd.shape, lambda i, j: (0, 0)),
        ],
        out_specs=pl.BlockSpec((1, tm, d), lambda i, j: (i, j, 0)),
        out_shape=jax.ShapeDtypeStruct((b, n_tiles * tm, d), F32),
        compiler_params=_params("parallel", "parallel"),
        name="merge_ffn",
    )(xs, oaf, oab, og, yb, yc, gt, modsel, gn, wbr, wo, norm_w, wgu, wd)


def _rope_tables(t, n_ctx):
    pos = jnp.arange(t)
    row = (pos // GRID_W).astype(F32)
    colp = (pos % GRID_W).astype(F32)
    n = HEAD_DIM // 4
    inv = ROPE_THETA ** (-jnp.arange(n, dtype=F32) / n)
    ang = jnp.concatenate([row[:, None] * inv, colp[:, None] * inv], axis=-1)
    cos = jnp.cos(ang)
    sin = jnp.sin(ang)
    reps = LANES // HEAD_DIM
    cos_t = jnp.tile(jnp.concatenate([cos, cos], axis=-1), (1, reps))
    sin_t = jnp.tile(jnp.concatenate([-sin, sin], axis=-1), (1, reps))
    cos_t = jnp.concatenate([jnp.ones((n_ctx, LANES), F32), cos_t], axis=0)
    sin_t = jnp.concatenate([jnp.zeros((n_ctx, LANES), F32), sin_t], axis=0)
    return cos_t, sin_t


def _deinterleave_heads(a):
    lead = a.shape[:-1]
    n = a.shape[-1] // HEAD_DIM
    a = a.reshape(lead + (n, HEAD_DIM // 2, 2))
    return jnp.swapaxes(a, -1, -2).reshape(lead + (n * HEAD_DIM,))


def _inproj_weight(w):
    b0 = 5 * A_HEADS * A_DK
    b1 = b0 + (B_HEADS + B_KV_HEADS) * HEAD_DIM
    return jnp.concatenate([w[:, :b0], _deinterleave_heads(w[:, b0:b1]), w[:, b1:]],
                           axis=1).astype(BF16)


def kernel(x, c, ctx, c_ctx, w_mod, b_mod, norm_mix, norm_ffn, w_in, lb_raw, gn_a, qn_b, kn_b,
           qn_c, kn_c, rel_bias, w_branch, w_out, w_gate_up, w_down):
    bn, t, d = x.shape
    n_ctx = ctx.shape[1]
    depth = w_mod.shape[0]
    tm = TOKEN_TILE
    assert n_ctx % tm == 0 and t % tm == 0 and t % GRID_W == 0
    n_ctx_tiles = n_ctx // tm
    aw = A_HEADS * A_DK

    c_all = jnp.zeros((8, d), F32).at[:bn].set(c).at[bn].set(c_ctx)
    mod = _modulation(c_all, w_mod, b_mod)
    modsel = jnp.stack([jnp.broadcast_to(mod[:, bn:bn + 1], (depth, bn, 6 * d)), mod[:, :bn]],
                       axis=2)[:, :, :, None, :]

    lbp = jax.nn.softmax(lb_raw.astype(F32), axis=0)
    lb_all = jnp.clip(jnp.cumsum(lbp, axis=0) - lbp[:1], 0.0, 1.0 - 1e-6)
    zeros2 = jnp.zeros_like(lb_all)
    lbc_all = jnp.concatenate([jnp.log(lb_all + LB_TINY), jnp.log1p(-lb_all), 1.0 - lb_all, zeros2],
                              axis=1)

    cos_t, sin_t = _rope_tables(t, n_ctx)
    lane = np.arange(LANES)
    gmat = jnp.asarray((lane[:, None] // HEAD_DIM == lane[None, :] // HEAD_DIM) / HEAD_DIM, BF16)
    dmat, smask, qrole = _hgrn_tables()
    tables = (jnp.asarray(dmat, BF16), jnp.asarray(smask), jnp.asarray(qrole))

    xs = jnp.concatenate([ctx, x], axis=1)
    for l in range(depth):
        last = l == depth - 1
        w_bf = _inproj_weight(w_in[l])
        headw = jnp.stack([jnp.tile(_deinterleave_heads(qn_b[l]), B_HEADS),
                           jnp.tile(_deinterleave_heads(kn_b[l]), B_HEADS),
                           jnp.tile(qn_c[l], C_HEADS), jnp.tile(kn_c[l], C_HEADS)]).astype(F32)
        headw = jnp.concatenate([headw, jnp.zeros((4, headw.shape[1]), F32)], axis=0)
        (aq, alff, akf, alfb, akb, av, aog, bq, bk, bv, cq, ck, cv, gt) = _in_projection(
            xs, modsel[l], norm_mix[l][None, :], w_bf, lbc_all[l], headw, cos_t, sin_t, gmat,
            n_ctx_tiles)
        oaf, oab = _hgrn(aq, alff, akf, alfb, akb, av, tables, n_ctx)
        yb = _gqa(bq, bk, bv, n_ctx)
        yc = _natten(cq, ck, cv, _natten_bias(rel_bias[l]), n_ctx)
        tile0 = n_ctx_tiles if last else 0
        n_tiles = (t if last else t + n_ctx) // tm
        xs = _merge_ffn(xs, oaf, oab, aog, yb, yc, gt, modsel[l], gn_a[l][None, :],
                        w_branch[l].astype(BF16), w_out[l].astype(BF16), norm_ffn[l][None, :],
                        w_gate_up[l].astype(BF16), w_down[l].astype(BF16),
                        n_ctx_tiles, tile0, n_tiles)
    return xs
```

```python
import functools

import numpy as np
import jax
import jax.numpy as jnp
from jax import lax
from jax.experimental import pallas as pl
from jax.experimental.pallas import tpu as pltpu

GRID_W = 64
A_HEADS = 4
A_DK = 128
B_HEADS = 8
B_KV_HEADS = 2
B_GROUP = B_HEADS // B_KV_HEADS
HEAD_DIM = 64
C_HEADS = 8
WIN_R = 8
WIN_C = 16
N_BRANCH = 3
ROPE_THETA = 10000.0
EPS = 1e-6
LB_TINY = 1e-30

LANES = 128
TOKEN_TILE = 256
A_CHUNK = 64
A_LEVELS = 6
KV_TILE = 512
Q_TILE = 256
NAT_ROWS = 4
NEG_BIG = -1e30
LOG2_E = 1.4426950408889634
BOUND_SLACK = 1.02
MAX_SAFE_SHIFT = 50.0
VMEM_LIMIT = 56 * 1024 * 1024

F32 = jnp.float32
BF16 = jnp.bfloat16


def _dot(a, b):
    return jnp.dot(a, b, preferred_element_type=F32)


def _dot_nt(a, b):
    return lax.dot_general(a, b, (((1,), (1,)), ((), ())), preferred_element_type=F32)


def _dot_tn(a, b):
    return lax.dot_general(a, b, (((0,), (0,)), ((), ())), preferred_element_type=F32)


def _split_bf16(x):
    hi = x.astype(BF16)
    lo = (x - hi.astype(F32)).astype(BF16)
    return hi, lo


def _silu(x):
    return x * jax.nn.sigmoid(x)


def _params(*sem):
    return pltpu.CompilerParams(dimension_semantics=sem, vmem_limit_bytes=VMEM_LIMIT)


def _resident(shape, index_map):
    return pl.BlockSpec(shape, index_map, pipeline_mode=pl.Buffered(1))


def _mod_kernel(c_ref, w_ref, b_ref, o_ref):
    a = _silu(c_ref[...])
    a_hi, a_lo = _split_bf16(a)
    w = w_ref[0]
    w_hi, w_lo = _split_bf16(w)
    acc = _dot(a_hi, w_hi) + _dot(a_hi, w_lo) + _dot(a_lo, w_hi)
    o_ref[0] = acc + b_ref[0]


def _modulation(c_all, w_mod, b_mod):
    depth, d, n = w_mod.shape
    rows = c_all.shape[0]
    tn = 1536
    return pl.pallas_call(
        _mod_kernel,
        grid=(depth, n // tn),
        in_specs=[
            pl.BlockSpec((rows, d), lambda l, j: (0, 0)),
            pl.BlockSpec((1, d, tn), lambda l, j: (l, 0, j)),
            pl.BlockSpec((1, 1, tn), lambda l, j: (l, 0, j)),
        ],
        out_specs=pl.BlockSpec((1, rows, tn), lambda l, j: (l, 0, j)),
        out_shape=jax.ShapeDtypeStruct((depth, rows, n), F32),
        compiler_params=_params("arbitrary", "arbitrary"),
        name="modulation",
    )(c_all, w_mod, b_mod.reshape(depth, 1, n))


def _group_mean_sq(x, gmat):
    hi, lo = _split_bf16(x * x)
    return _dot(jnp.concatenate([hi, lo], axis=-1), gmat)


def _head_norm(x, gmat, w):
    return x * lax.rsqrt(_group_mean_sq(x, gmat) + EPS) * w


def _rope(x, cos, sin_signed, first_half):
    rot = jnp.where(first_half, pltpu.roll(x, LANES - HEAD_DIM // 2, 1),
                    pltpu.roll(x, HEAD_DIM // 2, 1))
    return x * cos + rot * sin_signed


def _log_forget(z, la, l1):
    ls = jnp.minimum(z, 0.0) - jnp.log1p(jnp.exp(-jnp.abs(z)))
    t = l1 + ls
    return jnp.maximum(la, t) + jnp.log1p(jnp.exp(-jnp.abs(la - t)))


def _stream_tile(xc_ref, xl_ref, tile, n_ctx_tiles):
    return jnp.where(tile < n_ctx_tiles, xc_ref[0], xl_ref[0])


def _stream_specs(tm, d, n_ctx_tiles, lat_base, tile0=0):
    ctx = pl.BlockSpec((1, tm, d), lambda i, j: (i, jnp.minimum(j + tile0, n_ctx_tiles - 1), 0))
    lat = pl.BlockSpec((1, tm, d),
                       lambda i, j: (i, jnp.maximum(j + tile0 - n_ctx_tiles, 0) + lat_base, 0))
    return ctx, lat


def _inproj_kernel(xc_ref, xl_ref, mod_ref, nw_ref, w_ref, lbc_ref, hw_ref, cos_ref, sin_ref, gmat_ref,
                   aq_ref, alff_ref, akf_ref, alfb_ref, akb_ref, av_ref, aog_ref,
                   bq_ref, bk_ref, bv_ref, cq_ref, ck_ref, cv_ref, gt_ref, *, d, n_ctx_tiles):
    x = _stream_tile(xc_ref, xl_ref, pl.program_id(1), n_ctx_tiles)
    ms = jnp.mean(x * x, axis=-1, keepdims=True)
    y = x * lax.rsqrt(ms + EPS) * nw_ref[...]
    sh1 = mod_ref[0, 0, :, 0:d]
    sc1 = mod_ref[0, 0, :, d:2 * d]
    u = (y * (1.0 + sc1) + sh1).astype(BF16)

    aw = A_HEADS * A_DK
    col = [0]

    def proj(width):
        c0 = col[0]
        col[0] = c0 + width
        return _dot(u, w_ref[:, c0:c0 + width])

    col[0] = w_ref.shape[1] - N_BRANCH * d
    for k in range(N_BRANCH):
        gt_ref[0, :, k * d:(k + 1) * d] = jax.nn.sigmoid(proj(d)).astype(BF16)
    col[0] = 0

    aq_ref[0] = (proj(aw) * (A_DK ** -0.5)).astype(BF16)
    for lf_ref, k_ref, row in ((alff_ref, akf_ref, 0), (alfb_ref, akb_ref, 1)):
        z = proj(aw)
        la = lbc_ref[row:row + 1, :]
        l1 = lbc_ref[2 + row:3 + row, :]
        oml = lbc_ref[4 + row:5 + row, :]
        lf_ref[0] = _log_forget(z, la, l1)
        k_ref[0] = (oml * jax.nn.sigmoid(-z)).astype(BF16)
    av_ref[0] = proj(aw).astype(BF16)
    aog_ref[0] = proj(aw).astype(BF16)

    gmat = gmat_ref[...]
    lane = lax.broadcasted_iota(jnp.int32, (x.shape[0], LANES), 1)
    first_half = (lane % HEAD_DIM) < (HEAD_DIM // 2)
    cos = cos_ref[...]
    sin = sin_ref[...]
    scale = HEAD_DIM ** -0.5

    zq = proj(B_HEADS * HEAD_DIM)
    for j in range(B_HEADS * HEAD_DIM // LANES):
        sl = slice(j * LANES, (j + 1) * LANES)
        qn = _head_norm(zq[:, sl], gmat, hw_ref[0:1, sl])
        bq_ref[0, :, sl] = (_rope(qn, cos, sin, first_half) * (scale * LOG2_E)).astype(BF16)
    zk = proj(B_KV_HEADS * HEAD_DIM)
    kn = _head_norm(zk, gmat, hw_ref[1:2, 0:LANES])
    bk_ref[0] = _rope(kn, cos, sin, first_half).astype(BF16)
    bv_ref[0] = proj(B_KV_HEADS * HEAD_DIM).astype(BF16)

    zq = proj(C_HEADS * HEAD_DIM)
    for j in range(C_HEADS * HEAD_DIM // LANES):
        sl = slice(j * LANES, (j + 1) * LANES)
        cq_ref[0, :, sl] = (_head_norm(zq[:, sl], gmat, hw_ref[2:3, sl]) * scale).astype(BF16)
    zk = proj(C_HEADS * HEAD_DIM)
    for j in range(C_HEADS * HEAD_DIM // LANES):
        sl = slice(j * LANES, (j + 1) * LANES)
        ck_ref[0, :, sl] = _head_norm(zk[:, sl], gmat, hw_ref[3:4, sl]).astype(BF16)
    cv_ref[0] = proj(C_HEADS * HEAD_DIM).astype(BF16)


def _in_projection(x_ctx, x_lat, lat_base, s, modsel, norm_w, w_bf, lbc, headw, cos_t, sin_t, gmat,
                   n_ctx_tiles):
    b, _, d = x_ctx.shape
    tm = TOKEN_TILE
    aw = A_HEADS * A_DK
    n_in = w_bf.shape[1]
    widths = [(aw, BF16), (aw, F32), (aw, BF16), (aw, F32), (aw, BF16), (aw, BF16), (aw, BF16),
              (B_HEADS * HEAD_DIM, BF16), (B_KV_HEADS * HEAD_DIM, BF16), (B_KV_HEADS * HEAD_DIM, BF16),
              (C_HEADS * HEAD_DIM, BF16), (C_HEADS * HEAD_DIM, BF16), (C_HEADS * HEAD_DIM, BF16),
              (N_BRANCH * d, BF16)]
    tok = lambda w: pl.BlockSpec((1, tm, w), lambda i, j: (i, j, 0))
    return pl.pallas_call(
        functools.partial(_inproj_kernel, d=d, n_ctx_tiles=n_ctx_tiles),
        grid=(b, s // tm),
        in_specs=[
            *_stream_specs(tm, d, n_ctx_tiles, lat_base),
            pl.BlockSpec((1, 1, 1, modsel.shape[-1]),
                         lambda i, j: (i, (j >= n_ctx_tiles).astype(jnp.int32), 0, 0)),
            _resident((1, d), lambda i, j: (0, 0)),
            _resident((d, n_in), lambda i, j: (0, 0)),
            _resident(lbc.shape, lambda i, j: (0, 0)),
            _resident(headw.shape, lambda i, j: (0, 0)),
            pl.BlockSpec((tm, LANES), lambda i, j: (j, 0)),
            pl.BlockSpec((tm, LANES), lambda i, j: (j, 0)),
            _resident(gmat.shape, lambda i, j: (0, 0)),
        ],
        out_specs=[tok(w) for w, _ in widths],
        out_shape=[jax.ShapeDtypeStruct((b, s, w), dt) for w, dt in widths],
        compiler_params=_params("parallel", "parallel"),
        name="in_projection",
    )(x_ctx, x_lat, modsel, norm_w, w_bf, lbc, headw, cos_t, sin_t, gmat)


def _hgrn_tables():
    L = A_CHUNK
    ms = [1 << i for i in range(A_LEVELS)]
    dmat = np.zeros((2, (2 + A_LEVELS) * L, L), np.float32)
    smask = np.zeros((2, A_LEVELS + 1, L, L), np.float32)
    qrole = np.zeros((2, A_LEVELS, L, LANES), np.float32)
    t = np.arange(L)
    for dr in range(2):
        fwd = dr == 0
        for r in range(L):
            if fwd:
                dmat[dr, r, :r + 1] = 1.0
                dmat[dr, L + r, r + 1:] = 1.0
            else:
                dmat[dr, r, r:] = 1.0
                dmat[dr, L + r, :r] = 1.0
        for li, m in enumerate(ms):
            base = (2 + li) * L
            for r in range(L):
                start = (r // (2 * m)) * 2 * m
                mid = start + m
                upper = r >= mid
                if fwd:
                    if upper:
                        dmat[dr, base + r, mid:r + 1] = 1.0
                    else:
                        dmat[dr, base + r, r + 1:mid] = 1.0
                else:
                    if not upper:
                        dmat[dr, base + r, r:mid] = 1.0
                    else:
                        dmat[dr, base + r, mid:r] = 1.0
                qrole[dr, li, r, :] = 1.0 if (upper == fwd) else 0.0
            same = (t[:, None] // (2 * m)) == (t[None, :] // (2 * m))
            t_up = (t[:, None] % (2 * m)) >= m
            s_up = (t[None, :] % (2 * m)) >= m
            if fwd:
                smask[dr, li] = same & t_up & ~s_up
            else:
                smask[dr, li] = same & ~t_up & s_up
        smask[dr, A_LEVELS] = np.eye(L)
    return dmat, smask, qrole


def _hgrn_kernel(qf_ref, lff_ref, kf_ref, vf_ref, qb_ref, lfb_ref, kb_ref, vb_ref,
                 dmat_ref, smask_ref, qrole_ref, of_ref, ob_ref, stf_ref, stb_ref, *, n_chunks):
    L = A_CHUNK

    @pl.when(pl.program_id(1) == 0)
    def _():
        stf_ref[...] = jnp.zeros_like(stf_ref)
        stb_ref[...] = jnp.zeros_like(stb_ref)

    dirs = ((0, qf_ref, lff_ref, kf_ref, vf_ref, of_ref, stf_ref),
            (1, qb_ref, lfb_ref, kb_ref, vb_ref, ob_ref, stb_ref))

    def body(i, carry):
        chains = []
        for dr, q_ref, lf_ref, k_ref, v_ref, o_ref, st_ref in dirs:
            c = i if dr == 0 else n_chunks - 1 - i
            rows = pl.ds(pl.multiple_of(c * L, L), L)
            dm = dmat_ref[dr]
            for hp in range(A_HEADS // 2):
                g_hi, g_lo = _split_bf16(lf_ref[0, rows, 2 * hp * A_DK:(2 * hp + 2) * A_DK])
                ex2 = jnp.exp(_dot(dm, jnp.concatenate([g_hi, g_lo], axis=0)))
                for hh in range(2):
                    h = 2 * hp + hh
                    chains.append((dr, h, rows, ex2[:, hh * A_DK:(hh + 1) * A_DK],
                                   q_ref, k_ref, v_ref, o_ref, st_ref))
        staged = []
        for dr, h, rows, ex, q_ref, k_ref, v_ref, o_ref, st_ref in chains:
            cols = slice(h * A_DK, (h + 1) * A_DK)
            q = q_ref[0, rows, cols].astype(F32)
            k = k_ref[0, rows, cols].astype(F32)
            vb = v_ref[0, rows, cols]
            eb = ex[0:L]
            ebl = eb[L - 1:L] if dr == 0 else eb[0:1]
            st = st_ref[h]
            o = _dot_nt((q * eb).astype(BF16), st.astype(BF16))
            st_ref[h] = st * ebl + _dot_tn(vb, (k * ex[L:2 * L]).astype(BF16))
            parts = [jnp.sum(q * k, axis=-1, keepdims=True)]
            for li in range(A_LEVELS):
                role = qrole_ref[dr, li] > 0.5
                xl = (jnp.where(role, q, k) * ex[(2 + li) * L:(3 + li) * L]).astype(BF16)
                parts.append(_dot_nt(xl, xl))
            staged.append((dr, rows, cols, o, parts, vb, o_ref))
        for dr, rows, cols, o, parts, vb, o_ref in staged:
            scores = parts[0] * smask_ref[dr, A_LEVELS]
            for li in range(A_LEVELS):
                scores = scores + parts[1 + li] * smask_ref[dr, li]
            o_ref[0, rows, cols] = o + _dot(scores.astype(BF16), vb)
        return carry

    lax.fori_loop(0, n_chunks, body, 0)


def _hgrn(aq, alff, akf, alfb, akb, av, tables, n_ctx):
    b, s, w = aq.shape
    ts = TOKEN_TILE
    n_tiles = s // ts
    n_ctx_tiles = n_ctx // ts
    dmat, smask, qrole = tables

    def bwd_tile(j):
        return jnp.where(j < n_ctx_tiles, n_ctx_tiles - 1 - j, n_tiles - 1 - (j - n_ctx_tiles))

    fwd = lambda: pl.BlockSpec((1, ts, w), lambda i, j: (i, j, 0))
    bwd = lambda: pl.BlockSpec((1, ts, w), lambda i, j: (i, bwd_tile(j), 0))
    const = lambda a: _resident(a.shape, lambda i, j: (0,) * a.ndim)
    state = pltpu.VMEM((A_HEADS, A_DK, A_DK), F32)
    return pl.pallas_call(
        functools.partial(_hgrn_kernel, n_chunks=ts // A_CHUNK),
        grid=(b, n_tiles),
        in_specs=[fwd(), fwd(), fwd(), fwd(), bwd(), bwd(), bwd(), bwd(),
                  const(dmat), const(smask), const(qrole)],
        out_specs=[fwd(), bwd()],
        out_shape=[jax.ShapeDtypeStruct((b, s, w), F32)] * 2,
        scratch_shapes=[state, state],
        compiler_params=_params("parallel", "arbitrary"),
        name="hgrn2_scan",
    )(aq, alff, akf, av, aq, alfb, akb, av, dmat, smask, qrole)


def _gqa_kernel(q_ref, k_ref, v_ref, o_ref, qs_ref, kmax_ref, *, n_ctx, s):
    tq = Q_TILE
    tk = KV_TILE
    j = pl.program_id(1)
    is_latent = j >= n_ctx // tq
    n_lat = (s - n_ctx) // tk
    lane = lax.broadcasted_iota(jnp.int32, (tq, LANES), 1)
    upper = lane >= HEAD_DIM
    ctx_rows = pl.ds(0, n_ctx)

    @pl.when(j == 0)
    def _():
        li = lax.broadcasted_iota(jnp.int32, (LANES, LANES), 0) // HEAD_DIM
        lj = lax.broadcasted_iota(jnp.int32, (LANES, LANES), 1) // HEAD_DIM
        head_sum = jnp.where(li == lj, 1.0, 0.0).astype(BF16)

        def body(c, mx):
            kk = k_ref[0, pl.ds(pl.multiple_of(c * tk, tk), tk), :].astype(F32)
            hi, lo = _split_bf16(kk * kk)
            return jnp.maximum(mx, _dot(hi, head_sum) + _dot(lo, head_sum))

        mx = lax.fori_loop(0, n_lat, body, jnp.zeros((tk, LANES), F32))
        kk = k_ref[0, pl.ds(s - n_ctx, n_ctx), :].astype(F32)
        hi, lo = _split_bf16(kk * kk)
        tail = _dot(hi, head_sum) + _dot(lo, head_sum)
        kmax_ref[...] = jnp.maximum(jnp.max(mx, axis=0, keepdims=True),
                                    jnp.max(tail, axis=0, keepdims=True))

    def lat_rows(c):
        return pl.ds(n_ctx + c * tk, tk)

    def scores(rows):
        return _dot_nt(qs_ref[...], k_ref[0, rows, :])

    def lane_max(sc):
        out = sc[:, 0:LANES]
        for jb in range(1, sc.shape[1] // LANES):
            out = jnp.maximum(out, sc[:, jb * LANES:(jb + 1) * LANES])
        return out

    def row_max(lane_wise):
        return jnp.broadcast_to(jnp.max(lane_wise, axis=-1, keepdims=True), lane_wise.shape)

    def probs(sc, mb):
        ps = []
        ls = None
        for jb in range(sc.shape[1] // LANES):
            pj = jnp.exp2(sc[:, jb * LANES:(jb + 1) * LANES] - mb)
            ls = pj if ls is None else ls + pj
            ps.append(pj.astype(BF16))
        return jnp.concatenate(ps, axis=-1), ls

    def finish(kv, ls, acc):
        out = acc * (1.0 / jnp.sum(ls, axis=-1, keepdims=True))
        pair_out = [None, None]
        for gi in range(B_GROUP):
            h = kv * B_GROUP + gi
            oh = out[gi * tq:(gi + 1) * tq, :]
            if h % 2 != kv:
                oh = pltpu.roll(oh, HEAD_DIM, 1)
            pair_out[h % 2] = oh
            if h % 2 == 1:
                o_ref[0, :, (h // 2) * LANES:(h // 2 + 1) * LANES] = jnp.where(
                    upper, pair_out[1], pair_out[0]).astype(BF16)

    def context_only(kv):
        sc = scores(ctx_rows)
        p, ls = probs(sc, row_max(lane_max(sc)))
        finish(kv, ls, _dot(p, v_ref[0, ctx_rows, :]))

    def exact_row_max():
        def body(c, mx):
            rows = pl.ds(pl.multiple_of(n_ctx + c * tk, LANES), tk)
            return jnp.maximum(mx, lane_max(scores(rows)))

        return row_max(lax.fori_loop(0, n_lat, body, lane_max(scores(ctx_rows))))

    def all_keys(kv, shift_fn):
        mb = shift_fn()
        sc_next = scores(ctx_rows)
        ls = None
        acc = None
        for c in range(-1, n_lat):
            rows = ctx_rows if c < 0 else lat_rows(c)
            sc = sc_next
            if c + 1 < n_lat:
                sc_next = scores(lat_rows(c + 1))
            p, lt = probs(sc, mb)
            pv = _dot(p, v_ref[0, rows, :])
            ls = lt if ls is None else ls + lt
            acc = pv if acc is None else acc + pv
        finish(kv, ls, acc)

    for kv in range(B_KV_HEADS):
        for gi in range(B_GROUP):
            h = kv * B_GROUP + gi
            blk = q_ref[0, :, (h // 2) * LANES:(h // 2 + 1) * LANES].astype(F32)
            if h % 2 != kv:
                blk = pltpu.roll(blk, HEAD_DIM, 1)
            keep = upper if kv == 1 else jnp.logical_not(upper)
            qs_ref[gi * tq:(gi + 1) * tq, :] = jnp.where(keep, blk, 0.0).astype(BF16)
        qsq = qs_ref[...].astype(F32)
        qn2 = jnp.sum(qsq * qsq, axis=-1, keepdims=True)
        kn2 = jnp.max(kmax_ref[:, kv * HEAD_DIM:(kv + 1) * HEAD_DIM], axis=-1, keepdims=True)
        bound = jnp.broadcast_to(jnp.sqrt(qn2 * kn2) * BOUND_SLACK, (B_GROUP * tq, LANES))
        bound_ok = jnp.max(bound) <= MAX_SAFE_SHIFT
        pl.when(jnp.logical_and(is_latent, bound_ok))(
            functools.partial(all_keys, kv, lambda: bound))
        pl.when(jnp.logical_and(is_latent, jnp.logical_not(bound_ok)))(
            functools.partial(all_keys, kv, exact_row_max))
        pl.when(jnp.logical_not(is_latent))(functools.partial(context_only, kv))


def _gqa(bq, bk, bv, n_ctx):
    b, s, w = bq.shape
    tq = Q_TILE
    return pl.pallas_call(
        functools.partial(_gqa_kernel, n_ctx=n_ctx, s=s),
        grid=(b, s // tq),
        in_specs=[
            pl.BlockSpec((1, tq, w), lambda i, j: (i, j, 0)),
            pl.BlockSpec((1, s, LANES), lambda i, j: (i, 0, 0)),
            pl.BlockSpec((1, s, LANES), lambda i, j: (i, 0, 0)),
        ],
        out_specs=pl.BlockSpec((1, tq, w), lambda i, j: (i, j, 0)),
        out_shape=jax.ShapeDtypeStruct((b, s, w), BF16),
        scratch_shapes=[pltpu.VMEM((B_GROUP * tq, LANES), BF16), pltpu.VMEM((1, LANES), F32)],
        compiler_params=_params("parallel", "arbitrary"),
        name="gqa_attention",
    )(bq, bk, bv)


def _natten_bias(rel_bias):
    h = rel_bias.shape[0]
    qcol = np.arange(GRID_W)
    c0 = np.clip(qcol - WIN_C // 2, 0, GRID_W - WIN_C)
    kcol = np.arange(GRID_W)
    valid = (kcol[None, :] >= c0[:, None]) & (kcol[None, :] < c0[:, None] + WIN_C)
    dc = kcol[None, :] - qcol[:, None] + (WIN_C - 1)
    pick = (dc[None, :, :] == np.arange(2 * WIN_C - 1)[:, None, None]) & valid[None]
    cols = jnp.einsum('hdi,iqk->hdqk', rel_bias.astype(F32), jnp.asarray(pick, F32),
                      precision=lax.Precision.HIGHEST)
    cols = jnp.where(valid[None, None], cols, NEG_BIG)
    per_class = [jnp.swapaxes(cols[:, WIN_R - 1 - c:2 * WIN_R - 1 - c], 1, 2) for c in range(WIN_R)]
    return jnp.stack(per_class, axis=1).reshape(h, WIN_R, GRID_W, WIN_R * GRID_W)


def _natten_kernel(q_ref, k_ref, v_ref, bias_ref, o_ref, *, n_ctx, grid_rows):
    w = GRID_W
    band = WIN_R * w
    lane = lax.broadcasted_iota(jnp.int32, (w, LANES), 1)
    upper = lane >= HEAD_DIM
    kc = k_ref[0, 0:n_ctx, :]
    vc = v_ref[0, 0:n_ctx, :]

    def heads(q, fn):
        outs = []
        for hh in range(2):
            keep = upper if hh == 1 else jnp.logical_not(upper)
            outs.append(fn(hh, jnp.where(keep, q, jnp.zeros_like(q))))
        return jnp.where(upper, outs[1], outs[0])

    def ctx_block(i, carry):
        rows = pl.ds(pl.multiple_of(i * w, w), w)

        def attend(hh, qh):
            sc = _dot_nt(qh, kc)
            p = jnp.exp(sc - jnp.max(sc, axis=-1, keepdims=True))
            return _dot(p.astype(BF16), vc) / jnp.sum(p, axis=-1, keepdims=True)

        o_ref[0, rows, :] = heads(q_ref[0, rows, :], attend).astype(BF16)
        return carry

    lax.fori_loop(0, n_ctx // w, ctx_block, 0)

    nr = NAT_ROWS

    def row_group(gi, carry):
        qrows = pl.ds(pl.multiple_of(n_ctx + gi * (nr * w), nr * w), nr * w)
        qg = q_ref[0, qrows, :]
        lane_g = lax.broadcasted_iota(jnp.int32, qg.shape, 1) >= HEAD_DIM
        qh = (jnp.where(lane_g, jnp.zeros_like(qg), qg), jnp.where(lane_g, qg, jnp.zeros_like(qg)))
        s_ctx = [_dot_nt(qh[hh], kc) for hh in range(2)]
        s_win = {}
        vbs = []
        for rr in range(nr):
            r = gi * nr + rr
            r0 = jnp.clip(r - WIN_R // 2, 0, grid_rows - WIN_R)
            cls = r - r0
            krows = pl.ds(pl.multiple_of(n_ctx + r0 * w, w), band)
            kb = k_ref[0, krows, :]
            vbs.append(v_ref[0, krows, :])
            for hh in range(2):
                s_win[rr, hh] = _dot_nt(qh[hh][rr * w:(rr + 1) * w], kb) + bias_ref[hh, cls]
        p_win = {}
        p_ctx = [[], []]
        den = {}
        for rr in range(nr):
            for hh in range(2):
                sw = s_win[rr, hh]
                sc = s_ctx[hh][rr * w:(rr + 1) * w]
                m = jnp.maximum(jnp.max(sw, axis=-1, keepdims=True),
                                jnp.max(sc, axis=-1, keepdims=True))
                pw = jnp.exp(sw - m)
                pc = jnp.exp(sc - m)
                den[rr, hh] = jnp.sum(pw, axis=-1, keepdims=True) + jnp.sum(pc, axis=-1, keepdims=True)
                p_win[rr, hh] = pw.astype(BF16)
                p_ctx[hh].append(pc.astype(BF16))
        o_ctx = [_dot(jnp.concatenate(p_ctx[hh], axis=0), vc) for hh in range(2)]
        o_win = {(rr, hh): _dot(p_win[rr, hh], vbs[rr]) for rr in range(nr) for hh in range(2)}
        for rr in range(nr):
            outs = [(o_win[rr, hh] + o_ctx[hh][rr * w:(rr + 1) * w]) / den[rr, hh] for hh in range(2)]
            rows = pl.ds(pl.multiple_of(n_ctx + (gi * nr + rr) * w, w), w)
            o_ref[0, rows, :] = jnp.where(upper, outs[1], outs[0]).astype(BF16)
        return carry

    lax.fori_loop(0, grid_rows // nr, row_group, 0)


def _natten(cq, ck, cv, bias, n_ctx):
    b, s, w = cq.shape
    grid_rows = (s - n_ctx) // GRID_W
    assert grid_rows >= WIN_R and grid_rows % NAT_ROWS == 0
    blk = lambda: pl.BlockSpec((1, s, LANES), lambda i, p: (i, 0, p))
    return pl.pallas_call(
        functools.partial(_natten_kernel, n_ctx=n_ctx, grid_rows=grid_rows),
        grid=(b, w // LANES),
        in_specs=[blk(), blk(), blk(),
                  pl.BlockSpec((2, WIN_R, GRID_W, WIN_R * GRID_W), lambda i, p: (p, 0, 0, 0))],
        out_specs=blk(),
        out_shape=jax.ShapeDtypeStruct((b, s, w), BF16),
        compiler_params=_params("parallel", "parallel"),
        name="neighborhood_attention",
    )(cq, ck, cv, bias)


def _merge_ffn_kernel(xc_ref, xl_ref, oaf_ref, oab_ref, og_ref, yb_ref, yc_ref, gt_ref, mod_ref, gn_ref, wbr_ref,
                      wo_ref, nw_ref, wgu_ref, wd_ref, o_ref, *, d, hidden, n_split, tile0,
                      n_ctx_tiles):
    ya = []
    for h in range(A_HEADS):
        sl = slice(h * A_DK, (h + 1) * A_DK)
        oh = oaf_ref[0, :, sl] + oab_ref[0, :, sl]
        yh = oh * lax.rsqrt(jnp.mean(oh * oh, axis=-1, keepdims=True) + EPS) * gn_ref[...]
        ya.append((yh * _silu(og_ref[0, :, sl].astype(F32))).astype(BF16))
    ys = (jnp.concatenate(ya, axis=-1), yb_ref[0], yc_ref[0])
    merged = None
    for k in range(N_BRANCH):
        term = gt_ref[0, :, k * d:(k + 1) * d].astype(F32) * _dot(ys[k], wbr_ref[k])
        merged = term if merged is None else merged + term
    g1 = mod_ref[0, 0, :, 2 * d:3 * d]
    x = _stream_tile(xc_ref, xl_ref, pl.program_id(1) + tile0, n_ctx_tiles)
    x = x + g1 * _dot(merged.astype(BF16), wo_ref[...])

    y = x * lax.rsqrt(jnp.mean(x * x, axis=-1, keepdims=True) + EPS) * nw_ref[...]
    sh2 = mod_ref[0, 0, :, 3 * d:4 * d]
    sc2 = mod_ref[0, 0, :, 4 * d:5 * d]
    g2 = mod_ref[0, 0, :, 5 * d:6 * d]
    h = (y * (1.0 + sc2) + sh2).astype(BF16)
    step = hidden // n_split
    acc = None
    for c in range(n_split):
        a = _dot(h, wgu_ref[:, c * step:(c + 1) * step])
        g = _dot(h, wgu_ref[:, hidden + c * step:hidden + (c + 1) * step])
        part = _dot((_silu(a) * g).astype(BF16), wd_ref[c * step:(c + 1) * step, :])
        acc = part if acc is None else acc + part
    o_ref[0] = x + g2 * acc


def _merge_ffn(x_ctx, x_lat, lat_base, oaf, oab, og, yb, yc, gt, modsel, gn, wbr, wo, norm_w, wgu, wd,
               n_ctx_tiles, tile0, n_tiles):
    b, _, d = x_ctx.shape
    tm = TOKEN_TILE
    hidden = wd.shape[0]
    n_split = 2
    assert hidden % (n_split * LANES) == 0
    tok = lambda w: pl.BlockSpec((1, tm, w), lambda i, j: (i, j + tile0, 0))
    return pl.pallas_call(
        functools.partial(_merge_ffn_kernel, d=d, hidden=hidden, n_split=n_split, tile0=tile0,
                          n_ctx_tiles=n_ctx_tiles),
        grid=(b, n_tiles),
        in_specs=[
            *_stream_specs(tm, d, n_ctx_tiles, lat_base, tile0), tok(oaf.shape[-1]), tok(oab.shape[-1]), tok(og.shape[-1]), tok(yb.shape[-1]), tok(yc.shape[-1]),
            tok(gt.shape[-1]),
            pl.BlockSpec((1, 1, 1, modsel.shape[-1]),
                         lambda i, j: (i, (j + tile0 >= n_ctx_tiles).astype(jnp.int32), 0, 0)),
            _resident((1, A_DK), lambda i, j: (0, 0)),
            _resident(wbr.shape, lambda i, j: (0, 0, 0)),
            _resident(wo.shape, lambda i, j: (0, 0)),
            _resident((1, d), lambda i, j: (0, 0)),
            _resident(wgu.shape, lambda i, j: (0, 0)),
            _resident(wd.shape, lambda i, j: (0, 0)),
        ],
        out_specs=pl.BlockSpec((1, tm, d), lambda i, j: (i, j, 0)),
        out_shape=jax.ShapeDtypeStruct((b, n_tiles * tm, d), F32),
        compiler_params=_params("parallel", "parallel"),
        name="merge_ffn",
    )(x_ctx, x_lat, oaf, oab, og, yb, yc, gt, modsel, gn, wbr, wo, norm_w, wgu, wd)


def _rope_tables(t, n_ctx):
    pos = jnp.arange(t)
    row = (pos // GRID_W).astype(F32)
    colp = (pos % GRID_W).astype(F32)
    n = HEAD_DIM // 4
    inv = ROPE_THETA ** (-jnp.arange(n, dtype=F32) / n)
    ang = jnp.concatenate([row[:, None] * inv, colp[:, None] * inv], axis=-1)
    cos = jnp.cos(ang)
    sin = jnp.sin(ang)
    reps = LANES // HEAD_DIM
    cos_t = jnp.tile(jnp.concatenate([cos, cos], axis=-1), (1, reps))
    sin_t = jnp.tile(jnp.concatenate([-sin, sin], axis=-1), (1, reps))
    cos_t = jnp.concatenate([jnp.ones((n_ctx, LANES), F32), cos_t], axis=0)
    sin_t = jnp.concatenate([jnp.zeros((n_ctx, LANES), F32), sin_t], axis=0)
    return cos_t, sin_t


def _deinterleave_heads(a):
    lead = a.shape[:-1]
    n = a.shape[-1] // HEAD_DIM
    a = a.reshape(lead + (n, HEAD_DIM // 2, 2))
    return jnp.swapaxes(a, -1, -2).reshape(lead + (n * HEAD_DIM,))


def _inproj_weight(w):
    b0 = 5 * A_HEADS * A_DK
    b1 = b0 + (B_HEADS + B_KV_HEADS) * HEAD_DIM
    return jnp.concatenate([w[:, :b0], _deinterleave_heads(w[:, b0:b1]), w[:, b1:]],
                           axis=1).astype(BF16)


def kernel(x, c, ctx, c_ctx, w_mod, b_mod, norm_mix, norm_ffn, w_in, lb_raw, gn_a, qn_b, kn_b,
           qn_c, kn_c, rel_bias, w_branch, w_out, w_gate_up, w_down):
    bn, t, d = x.shape
    n_ctx = ctx.shape[1]
    depth = w_mod.shape[0]
    tm = TOKEN_TILE
    assert n_ctx % tm == 0 and t % tm == 0 and t % GRID_W == 0
    n_ctx_tiles = n_ctx // tm
    aw = A_HEADS * A_DK

    c_all = jnp.zeros((8, d), F32).at[:bn].set(c).at[bn].set(c_ctx)
    mod = _modulation(c_all, w_mod, b_mod)
    modsel = jnp.stack([jnp.broadcast_to(mod[:, bn:bn + 1], (depth, bn, 6 * d)), mod[:, :bn]],
                       axis=2)[:, :, :, None, :]

    lbp = jax.nn.softmax(lb_raw.astype(F32), axis=0)
    lb_all = jnp.clip(jnp.cumsum(lbp, axis=0) - lbp[:1], 0.0, 1.0 - 1e-6)
    zeros2 = jnp.zeros_like(lb_all)
    lbc_all = jnp.concatenate([jnp.log(lb_all + LB_TINY), jnp.log1p(-lb_all), 1.0 - lb_all, zeros2],
                              axis=1)

    cos_t, sin_t = _rope_tables(t, n_ctx)
    lane = np.arange(LANES)
    gmat = (lane[:, None] // HEAD_DIM == lane[None, :] // HEAD_DIM) / HEAD_DIM
    gmat = jnp.asarray(np.concatenate([gmat, gmat], axis=0), BF16)
    dmat, smask, qrole = _hgrn_tables()
    tables = (jnp.asarray(np.concatenate([dmat, dmat], axis=2), BF16), jnp.asarray(smask),
              jnp.asarray(qrole))

    x_ctx, x_lat, lat_base = ctx, x, 0
    for l in range(depth):
        last = l == depth - 1
        w_bf = _inproj_weight(w_in[l])
        headw = jnp.stack([jnp.tile(_deinterleave_heads(qn_b[l]), B_HEADS),
                           jnp.tile(_deinterleave_heads(kn_b[l]), B_HEADS),
                           jnp.tile(qn_c[l], C_HEADS), jnp.tile(kn_c[l], C_HEADS)]).astype(F32)
        headw = jnp.concatenate([headw, jnp.zeros((4, headw.shape[1]), F32)], axis=0)
        (aq, alff, akf, alfb, akb, av, aog, bq, bk, bv, cq, ck, cv, gt) = _in_projection(
            x_ctx, x_lat, lat_base, t + n_ctx, modsel[l], norm_mix[l][None, :], w_bf, lbc_all[l],
            headw, cos_t, sin_t, gmat, n_ctx_tiles)
        oaf, oab = _hgrn(aq, alff, akf, alfb, akb, av, tables, n_ctx)
        yb = _gqa(bq, bk, bv, n_ctx)
        yc = _natten(cq, ck, cv, _natten_bias(rel_bias[l]), n_ctx)
        tile0 = n_ctx_tiles if last else 0
        n_tiles = (t if last else t + n_ctx) // tm
        xs = _merge_ffn(x_ctx, x_lat, lat_base, oaf, oab, aog, yb, yc, gt, modsel[l],
                        gn_a[l][None, :], w_branch[l].astype(BF16), w_out[l].astype(BF16),
                        norm_ffn[l][None, :], w_gate_up[l].astype(BF16), w_down[l].astype(BF16),
                        n_ctx_tiles, tile0, n_tiles)
        x_ctx, x_lat, lat_base = xs, xs, n_ctx_tiles
    return xs
```

```python
import functools

import numpy as np
import jax
import jax.numpy as jnp
from jax import lax
from jax.experimental import pallas as pl
from jax.experimental.pallas import tpu as pltpu

GRID_W = 64
A_HEADS = 4
A_DK = 128
B_HEADS = 8
B_KV_HEADS = 2
B_GROUP = B_HEADS // B_KV_HEADS
HEAD_DIM = 64
C_HEADS = 8
WIN_R = 8
WIN_C = 16
N_BRANCH = 3
ROPE_THETA = 10000.0
EPS = 1e-6
LB_TINY = 1e-30

LANES = 128
TOKEN_TILE = 256
A_CHUNK = 64
A_LEVELS = 6
A_BATCH = 1
KV_TILE = 512
Q_TILE = 256
NAT_ROWS = 8
NEG_BIG = -1e30
LOG2_E = 1.4426950408889634
BOUND_SLACK = 1.02
MAX_SAFE_SHIFT = 50.0
VMEM_LIMIT = 56 * 1024 * 1024

F32 = jnp.float32
BF16 = jnp.bfloat16


def _dot(a, b):
    return jnp.dot(a, b, preferred_element_type=F32)


def _dot_nt(a, b):
    return lax.dot_general(a, b, (((1,), (1,)), ((), ())), preferred_element_type=F32)


def _dot_tn(a, b):
    return lax.dot_general(a, b, (((0,), (0,)), ((), ())), preferred_element_type=F32)


def _split_bf16(x):
    hi = x.astype(BF16)
    lo = (x - hi.astype(F32)).astype(BF16)
    return hi, lo


def _silu(x):
    return x * jax.nn.sigmoid(x)


def _params(*sem):
    return pltpu.CompilerParams(dimension_semantics=sem, vmem_limit_bytes=VMEM_LIMIT)


def _resident(shape, index_map):
    return pl.BlockSpec(shape, index_map, pipeline_mode=pl.Buffered(1))


def _mod_kernel(c_ref, w_ref, b_ref, o_ref):
    a = _silu(c_ref[...])
    a_hi, a_lo = _split_bf16(a)
    w = w_ref[0]
    w_hi, w_lo = _split_bf16(w)
    acc = _dot(a_hi, w_hi) + _dot(a_hi, w_lo) + _dot(a_lo, w_hi)
    o_ref[0] = acc + b_ref[0]


def _modulation(c_all, w_mod, b_mod):
    depth, d, n = w_mod.shape
    rows = c_all.shape[0]
    tn = 1536
    return pl.pallas_call(
        _mod_kernel,
        grid=(depth, n // tn),
        in_specs=[
            pl.BlockSpec((rows, d), lambda l, j: (0, 0)),
            pl.BlockSpec((1, d, tn), lambda l, j: (l, 0, j)),
            pl.BlockSpec((1, 1, tn), lambda l, j: (l, 0, j)),
        ],
        out_specs=pl.BlockSpec((1, rows, tn), lambda l, j: (l, 0, j)),
        out_shape=jax.ShapeDtypeStruct((depth, rows, n), F32),
        compiler_params=_params("arbitrary", "arbitrary"),
        name="modulation",
    )(c_all, w_mod, b_mod.reshape(depth, 1, n))


def _group_mean_sq(x, gmat):
    hi, lo = _split_bf16(x * x)
    return _dot(jnp.concatenate([hi, lo], axis=-1), gmat)


def _head_norm(x, gmat, w):
    return x * lax.rsqrt(_group_mean_sq(x, gmat) + EPS) * w


def _rope(x, cos, sin_signed, first_half):
    rot = jnp.where(first_half, pltpu.roll(x, LANES - HEAD_DIM // 2, 1),
                    pltpu.roll(x, HEAD_DIM // 2, 1))
    return x * cos + rot * sin_signed


def _log_forget(z, la, l1):
    ls = jnp.minimum(z, 0.0) - jnp.log1p(jnp.exp(-jnp.abs(z)))
    t = l1 + ls
    return jnp.maximum(la, t) + jnp.log1p(jnp.exp(-jnp.abs(la - t)))


def _stream_tile(xc_ref, xl_ref, tile, n_ctx_tiles):
    return jnp.where(tile < n_ctx_tiles, xc_ref[0], xl_ref[0])


def _stream_specs(tm, d, n_ctx_tiles, lat_base, tile0=0):
    ctx = pl.BlockSpec((1, tm, d), lambda i, j: (i, jnp.minimum(j + tile0, n_ctx_tiles - 1), 0))
    lat = pl.BlockSpec((1, tm, d),
                       lambda i, j: (i, jnp.maximum(j + tile0 - n_ctx_tiles, 0) + lat_base, 0))
    return ctx, lat


def _inproj_kernel(xc_ref, xl_ref, mod_ref, nw_ref, w_ref, lbc_ref, hw_ref, cos_ref, sin_ref, gmat_ref,
                   aq_ref, alff_ref, akf_ref, alfb_ref, akb_ref, av_ref, aog_ref,
                   bq_ref, bk_ref, bv_ref, cq_ref, ck_ref, cv_ref, gt_ref, *, d, n_ctx_tiles):
    x = _stream_tile(xc_ref, xl_ref, pl.program_id(1), n_ctx_tiles)
    ms = jnp.mean(x * x, axis=-1, keepdims=True)
    y = x * lax.rsqrt(ms + EPS) * nw_ref[...]
    sh1 = mod_ref[0, 0, :, 0:d]
    sc1 = mod_ref[0, 0, :, d:2 * d]
    u = (y * (1.0 + sc1) + sh1).astype(BF16)

    aw = A_HEADS * A_DK
    col = [0]

    def proj(width):
        c0 = col[0]
        col[0] = c0 + width
        return _dot(u, w_ref[0, :, c0:c0 + width])

    col[0] = w_ref.shape[2] - N_BRANCH * d
    for k in range(N_BRANCH):
        gt_ref[0, :, k * d:(k + 1) * d] = jax.nn.sigmoid(proj(d)).astype(BF16)
    col[0] = 0

    aq_ref[0] = (proj(aw) * (A_DK ** -0.5)).astype(BF16)
    for lf_ref, k_ref, row in ((alff_ref, akf_ref, 0), (alfb_ref, akb_ref, 1)):
        z = proj(aw)
        la = lbc_ref[row:row + 1, :]
        l1 = lbc_ref[2 + row:3 + row, :]
        oml = lbc_ref[4 + row:5 + row, :]
        lf_ref[0] = _log_forget(z, la, l1)
        k_ref[0] = (oml * jax.nn.sigmoid(-z)).astype(BF16)
    av_ref[0] = proj(aw).astype(BF16)
    aog_ref[0] = proj(aw).astype(BF16)

    gmat = gmat_ref[...]
    lane = lax.broadcasted_iota(jnp.int32, (x.shape[0], LANES), 1)
    first_half = (lane % HEAD_DIM) < (HEAD_DIM // 2)
    cos = cos_ref[...]
    sin = sin_ref[...]
    scale = HEAD_DIM ** -0.5

    zq = proj(B_HEADS * HEAD_DIM)
    for j in range(B_HEADS * HEAD_DIM // LANES):
        sl = slice(j * LANES, (j + 1) * LANES)
        qn = _head_norm(zq[:, sl], gmat, hw_ref[0:1, sl])
        bq_ref[0, :, sl] = (_rope(qn, cos, sin, first_half) * (scale * LOG2_E)).astype(BF16)
    zk = proj(B_KV_HEADS * HEAD_DIM)
    kn = _head_norm(zk, gmat, hw_ref[1:2, 0:LANES])
    bk_ref[0] = _rope(kn, cos, sin, first_half).astype(BF16)
    bv_ref[0] = proj(B_KV_HEADS * HEAD_DIM).astype(BF16)

    zq = proj(C_HEADS * HEAD_DIM)
    for j in range(C_HEADS * HEAD_DIM // LANES):
        sl = slice(j * LANES, (j + 1) * LANES)
        cq_ref[0, :, sl] = (_head_norm(zq[:, sl], gmat, hw_ref[2:3, sl]) * scale).astype(BF16)
    zk = proj(C_HEADS * HEAD_DIM)
    for j in range(C_HEADS * HEAD_DIM // LANES):
        sl = slice(j * LANES, (j + 1) * LANES)
        ck_ref[0, :, sl] = _head_norm(zk[:, sl], gmat, hw_ref[3:4, sl]).astype(BF16)
    cv_ref[0] = proj(C_HEADS * HEAD_DIM).astype(BF16)


def _in_projection(x_ctx, x_lat, lat_base, s, modsel, norm_w, w_bf, layer, lbc, headw, cos_t, sin_t,
                   gmat, n_ctx_tiles):
    b, _, d = x_ctx.shape
    tm = TOKEN_TILE
    aw = A_HEADS * A_DK
    n_in = w_bf.shape[2]
    widths = [(aw, BF16), (aw, F32), (aw, BF16), (aw, F32), (aw, BF16), (aw, BF16), (aw, BF16),
              (B_HEADS * HEAD_DIM, BF16), (B_KV_HEADS * HEAD_DIM, BF16), (B_KV_HEADS * HEAD_DIM, BF16),
              (C_HEADS * HEAD_DIM, BF16), (C_HEADS * HEAD_DIM, BF16), (C_HEADS * HEAD_DIM, BF16),
              (N_BRANCH * d, BF16)]
    tok = lambda w: pl.BlockSpec((1, tm, w), lambda i, j: (i, j, 0))
    return pl.pallas_call(
        functools.partial(_inproj_kernel, d=d, n_ctx_tiles=n_ctx_tiles),
        grid=(b, s // tm),
        in_specs=[
            *_stream_specs(tm, d, n_ctx_tiles, lat_base),
            pl.BlockSpec((1, 1, 1, modsel.shape[-1]),
                         lambda i, j: (i, (j >= n_ctx_tiles).astype(jnp.int32), 0, 0)),
            _resident((1, d), lambda i, j: (0, 0)),
            _resident((1, d, n_in), lambda i, j: (layer, 0, 0)),
            _resident(lbc.shape, lambda i, j: (0, 0)),
            _resident(headw.shape, lambda i, j: (0, 0)),
            pl.BlockSpec((tm, LANES), lambda i, j: (j, 0)),
            pl.BlockSpec((tm, LANES), lambda i, j: (j, 0)),
            _resident(gmat.shape, lambda i, j: (0, 0)),
        ],
        out_specs=[tok(w) for w, _ in widths],
        out_shape=[jax.ShapeDtypeStruct((b, s, w), dt) for w, dt in widths],
        compiler_params=_params("parallel", "parallel"),
        name="in_projection",
    )(x_ctx, x_lat, modsel, norm_w, w_bf, lbc, headw, cos_t, sin_t, gmat)


def _hgrn_tables():
    L = A_CHUNK
    ms = [1 << i for i in range(A_LEVELS)]
    dmat = np.zeros((2, (2 + A_LEVELS) * L, L), np.float32)
    smask = np.zeros((2, A_LEVELS + 1, L, L), np.float32)
    qrole = np.zeros((2, A_LEVELS, L, LANES), np.float32)
    t = np.arange(L)
    for dr in range(2):
        fwd = dr == 0
        for r in range(L):
            if fwd:
                dmat[dr, r, :r + 1] = 1.0
                dmat[dr, L + r, r + 1:] = 1.0
            else:
                dmat[dr, r, r:] = 1.0
                dmat[dr, L + r, :r] = 1.0
        for li, m in enumerate(ms):
            base = (2 + li) * L
            for r in range(L):
                start = (r // (2 * m)) * 2 * m
                mid = start + m
                upper = r >= mid
                if fwd:
                    if upper:
                        dmat[dr, base + r, mid:r + 1] = 1.0
                    else:
                        dmat[dr, base + r, r + 1:mid] = 1.0
                else:
                    if not upper:
                        dmat[dr, base + r, r:mid] = 1.0
                    else:
                        dmat[dr, base + r, mid:r] = 1.0
                qrole[dr, li, r, :] = 1.0 if (upper == fwd) else 0.0
            same = (t[:, None] // (2 * m)) == (t[None, :] // (2 * m))
            t_up = (t[:, None] % (2 * m)) >= m
            s_up = (t[None, :] % (2 * m)) >= m
            if fwd:
                smask[dr, li] = same & t_up & ~s_up
            else:
                smask[dr, li] = same & ~t_up & s_up
        smask[dr, A_LEVELS] = np.eye(L)
    return dmat, smask, qrole


def _hgrn_kernel(qf_ref, lff_ref, kf_ref, vf_ref, qb_ref, lfb_ref, kb_ref, vb_ref,
                 dmat_ref, smask_ref, qrole_ref, of_ref, ob_ref, stf_ref, stb_ref, *, n_chunks,
                 n_batch):
    L = A_CHUNK

    @pl.when(pl.program_id(1) == 0)
    def _():
        stf_ref[...] = jnp.zeros_like(stf_ref)
        stb_ref[...] = jnp.zeros_like(stb_ref)

    dirs = ((0, qf_ref, lff_ref, kf_ref, vf_ref, of_ref, stf_ref),
            (1, qb_ref, lfb_ref, kb_ref, vb_ref, ob_ref, stb_ref))

    def body(i, carry):
        chains = []
        for bi in range(n_batch):
            for dr, q_ref, lf_ref, k_ref, v_ref, o_ref, st_ref in dirs:
                c = i if dr == 0 else n_chunks - 1 - i
                rows = pl.ds(pl.multiple_of(c * L, L), L)
                dm = dmat_ref[dr]
                for hp in range(A_HEADS // 2):
                    g_hi, g_lo = _split_bf16(lf_ref[bi, rows, 2 * hp * A_DK:(2 * hp + 2) * A_DK])
                    ex2 = jnp.exp(_dot(dm, jnp.concatenate([g_hi, g_lo], axis=0)))
                    for hh in range(2):
                        h = 2 * hp + hh
                        chains.append((bi, dr, h, rows, ex2[:, hh * A_DK:(hh + 1) * A_DK],
                                       q_ref, k_ref, v_ref, o_ref, st_ref))
        staged = []
        for bi, dr, h, rows, ex, q_ref, k_ref, v_ref, o_ref, st_ref in chains:
            cols = slice(h * A_DK, (h + 1) * A_DK)
            q = q_ref[bi, rows, cols].astype(F32)
            k = k_ref[bi, rows, cols].astype(F32)
            vb = v_ref[bi, rows, cols]
            eb = ex[0:L]
            ebl = eb[L - 1:L] if dr == 0 else eb[0:1]
            st = st_ref[bi * A_HEADS + h]
            o = _dot_nt((q * eb).astype(BF16), st.astype(BF16))
            st_ref[bi * A_HEADS + h] = st * ebl + _dot_tn(vb, (k * ex[L:2 * L]).astype(BF16))
            parts = [jnp.sum(q * k, axis=-1, keepdims=True)]
            for li in range(A_LEVELS):
                role = qrole_ref[dr, li] > 0.5
                xl = (jnp.where(role, q, k) * ex[(2 + li) * L:(3 + li) * L]).astype(BF16)
                parts.append(_dot_nt(xl, xl))
            staged.append((bi, dr, rows, cols, o, parts, vb, o_ref))
        for bi, dr, rows, cols, o, parts, vb, o_ref in staged:
            scores = parts[0] * smask_ref[dr, A_LEVELS]
            for li in range(A_LEVELS):
                scores = scores + parts[1 + li] * smask_ref[dr, li]
            o_ref[bi, rows, cols] = o + _dot(scores.astype(BF16), vb)
        return carry

    lax.fori_loop(0, n_chunks, body, 0)


def _hgrn(aq, alff, akf, alfb, akb, av, tables, n_ctx):
    b, s, w = aq.shape
    ts = TOKEN_TILE
    n_tiles = s // ts
    n_ctx_tiles = n_ctx // ts
    dmat, smask, qrole = tables

    def bwd_tile(j):
        return jnp.where(j < n_ctx_tiles, n_ctx_tiles - 1 - j, n_tiles - 1 - (j - n_ctx_tiles))

    nb = A_BATCH if b % A_BATCH == 0 else 1
    fwd = lambda: pl.BlockSpec((nb, ts, w), lambda i, j: (i, j, 0))
    bwd = lambda: pl.BlockSpec((nb, ts, w), lambda i, j: (i, bwd_tile(j), 0))
    const = lambda a: _resident(a.shape, lambda i, j: (0,) * a.ndim)
    state = pltpu.VMEM((nb * A_HEADS, A_DK, A_DK), F32)
    return pl.pallas_call(
        functools.partial(_hgrn_kernel, n_chunks=ts // A_CHUNK, n_batch=nb),
        grid=(b // nb, n_tiles),
        in_specs=[fwd(), fwd(), fwd(), fwd(), bwd(), bwd(), bwd(), bwd(),
                  const(dmat), const(smask), const(qrole)],
        out_specs=[fwd(), bwd()],
        out_shape=[jax.ShapeDtypeStruct((b, s, w), F32)] * 2,
        scratch_shapes=[state, state],
        compiler_params=_params("parallel", "arbitrary"),
        name="hgrn2_scan",
    )(aq, alff, akf, av, aq, alfb, akb, av, dmat, smask, qrole)


def _gqa_kernel(q_ref, k_ref, v_ref, o_ref, qs_ref, kmax_ref, *, n_ctx, s):
    tq = Q_TILE
    tk = KV_TILE
    j = pl.program_id(1)
    is_latent = j >= n_ctx // tq
    n_lat = (s - n_ctx) // tk
    lane = lax.broadcasted_iota(jnp.int32, (tq, LANES), 1)
    upper = lane >= HEAD_DIM
    ctx_rows = pl.ds(0, n_ctx)

    @pl.when(j == 0)
    def _():
        li = lax.broadcasted_iota(jnp.int32, (LANES, LANES), 0) // HEAD_DIM
        lj = lax.broadcasted_iota(jnp.int32, (LANES, LANES), 1) // HEAD_DIM
        head_sum = jnp.where(li == lj, 1.0, 0.0).astype(BF16)

        def body(c, mx):
            kk = k_ref[0, pl.ds(pl.multiple_of(c * tk, tk), tk), :].astype(F32)
            hi, lo = _split_bf16(kk * kk)
            return jnp.maximum(mx, _dot(hi, head_sum) + _dot(lo, head_sum))

        mx = lax.fori_loop(0, n_lat, body, jnp.zeros((tk, LANES), F32))
        kk = k_ref[0, pl.ds(s - n_ctx, n_ctx), :].astype(F32)
        hi, lo = _split_bf16(kk * kk)
        tail = _dot(hi, head_sum) + _dot(lo, head_sum)
        kmax_ref[...] = jnp.maximum(jnp.max(mx, axis=0, keepdims=True),
                                    jnp.max(tail, axis=0, keepdims=True))

    def lat_rows(c):
        return pl.ds(n_ctx + c * tk, tk)

    def scores(rows):
        return _dot_nt(qs_ref[...], k_ref[0, rows, :])

    def lane_max(sc):
        out = sc[:, 0:LANES]
        for jb in range(1, sc.shape[1] // LANES):
            out = jnp.maximum(out, sc[:, jb * LANES:(jb + 1) * LANES])
        return out

    def row_max(lane_wise):
        return jnp.broadcast_to(jnp.max(lane_wise, axis=-1, keepdims=True), lane_wise.shape)

    def probs(sc, mb):
        ps = []
        ls = None
        for jb in range(sc.shape[1] // LANES):
            pj = jnp.exp2(sc[:, jb * LANES:(jb + 1) * LANES] - mb)
            ls = pj if ls is None else ls + pj
            ps.append(pj.astype(BF16))
        return jnp.concatenate(ps, axis=-1), ls

    def finish(kv, ls, acc):
        out = acc * (1.0 / jnp.sum(ls, axis=-1, keepdims=True))
        pair_out = [None, None]
        for gi in range(B_GROUP):
            h = kv * B_GROUP + gi
            oh = out[gi * tq:(gi + 1) * tq, :]
            if h % 2 != kv:
                oh = pltpu.roll(oh, HEAD_DIM, 1)
            pair_out[h % 2] = oh
            if h % 2 == 1:
                o_ref[0, :, (h // 2) * LANES:(h // 2 + 1) * LANES] = jnp.where(
                    upper, pair_out[1], pair_out[0]).astype(BF16)

    def context_only(kv):
        sc = scores(ctx_rows)
        p, ls = probs(sc, row_max(lane_max(sc)))
        finish(kv, ls, _dot(p, v_ref[0, ctx_rows, :]))

    def exact_row_max():
        def body(c, mx):
            rows = pl.ds(pl.multiple_of(n_ctx + c * tk, LANES), tk)
            return jnp.maximum(mx, lane_max(scores(rows)))

        return row_max(lax.fori_loop(0, n_lat, body, lane_max(scores(ctx_rows))))

    def all_keys(kv, shift_fn):
        mb = shift_fn()
        sc_next = scores(ctx_rows)
        ls = None
        acc = None
        for c in range(-1, n_lat):
            rows = ctx_rows if c < 0 else lat_rows(c)
            sc = sc_next
            if c + 1 < n_lat:
                sc_next = scores(lat_rows(c + 1))
            p, lt = probs(sc, mb)
            pv = _dot(p, v_ref[0, rows, :])
            ls = lt if ls is None else ls + lt
            acc = pv if acc is None else acc + pv
        finish(kv, ls, acc)

    for kv in range(B_KV_HEADS):
        for gi in range(B_GROUP):
            h = kv * B_GROUP + gi
            blk = q_ref[0, :, (h // 2) * LANES:(h // 2 + 1) * LANES].astype(F32)
            if h % 2 != kv:
                blk = pltpu.roll(blk, HEAD_DIM, 1)
            keep = upper if kv == 1 else jnp.logical_not(upper)
            qs_ref[gi * tq:(gi + 1) * tq, :] = jnp.where(keep, blk, 0.0).astype(BF16)
        qsq = qs_ref[...].astype(F32)
        qn2 = jnp.sum(qsq * qsq, axis=-1, keepdims=True)
        kn2 = jnp.max(kmax_ref[:, kv * HEAD_DIM:(kv + 1) * HEAD_DIM], axis=-1, keepdims=True)
        bound = jnp.broadcast_to(jnp.sqrt(qn2 * kn2) * BOUND_SLACK, (B_GROUP * tq, LANES))
        bound_ok = jnp.max(bound) <= MAX_SAFE_SHIFT
        pl.when(jnp.logical_and(is_latent, bound_ok))(
            functools.partial(all_keys, kv, lambda: bound))
        pl.when(jnp.logical_and(is_latent, jnp.logical_not(bound_ok)))(
            functools.partial(all_keys, kv, exact_row_max))
        pl.when(jnp.logical_not(is_latent))(functools.partial(context_only, kv))


def _gqa(bq, bk, bv, n_ctx):
    b, s, w = bq.shape
    tq = Q_TILE
    return pl.pallas_call(
        functools.partial(_gqa_kernel, n_ctx=n_ctx, s=s),
        grid=(b, s // tq),
        in_specs=[
            pl.BlockSpec((1, tq, w), lambda i, j: (i, j, 0)),
            pl.BlockSpec((1, s, LANES), lambda i, j: (i, 0, 0)),
            pl.BlockSpec((1, s, LANES), lambda i, j: (i, 0, 0)),
        ],
        out_specs=pl.BlockSpec((1, tq, w), lambda i, j: (i, j, 0)),
        out_shape=jax.ShapeDtypeStruct((b, s, w), BF16),
        scratch_shapes=[pltpu.VMEM((B_GROUP * tq, LANES), BF16), pltpu.VMEM((1, LANES), F32)],
        compiler_params=_params("parallel", "arbitrary"),
        name="gqa_attention",
    )(bq, bk, bv)


def _natten_bias(rel_bias):
    h = rel_bias.shape[0]
    qcol = np.arange(GRID_W)
    c0 = np.clip(qcol - WIN_C // 2, 0, GRID_W - WIN_C)
    kcol = np.arange(GRID_W)
    valid = (kcol[None, :] >= c0[:, None]) & (kcol[None, :] < c0[:, None] + WIN_C)
    dc = kcol[None, :] - qcol[:, None] + (WIN_C - 1)
    pick = (dc[None, :, :] == np.arange(2 * WIN_C - 1)[:, None, None]) & valid[None]
    cols = jnp.einsum('hdi,iqk->hdqk', rel_bias.astype(F32), jnp.asarray(pick, F32),
                      precision=lax.Precision.HIGHEST)
    cols = jnp.where(valid[None, None], cols, NEG_BIG)
    per_class = [jnp.swapaxes(cols[:, WIN_R - 1 - c:2 * WIN_R - 1 - c], 1, 2) for c in range(WIN_R)]
    return jnp.stack(per_class, axis=1).reshape(h, WIN_R, GRID_W, WIN_R * GRID_W)


def _natten_kernel(q_ref, k_ref, v_ref, bias_ref, o_ref, *, n_ctx, grid_rows):
    w = GRID_W
    band = WIN_R * w
    lane = lax.broadcasted_iota(jnp.int32, (w, LANES), 1)
    upper = lane >= HEAD_DIM
    kc = k_ref[0, 0:n_ctx, :]
    vc = v_ref[0, 0:n_ctx, :]

    def heads(q, fn):
        outs = []
        for hh in range(2):
            keep = upper if hh == 1 else jnp.logical_not(upper)
            outs.append(fn(hh, jnp.where(keep, q, jnp.zeros_like(q))))
        return jnp.where(upper, outs[1], outs[0])

    def ctx_block(i, carry):
        rows = pl.ds(pl.multiple_of(i * w, w), w)

        def attend(hh, qh):
            sc = _dot_nt(qh, kc)
            p = jnp.exp(sc - jnp.max(sc, axis=-1, keepdims=True))
            return _dot(p.astype(BF16), vc) / jnp.sum(p, axis=-1, keepdims=True)

        o_ref[0, rows, :] = heads(q_ref[0, rows, :], attend).astype(BF16)
        return carry

    lax.fori_loop(0, n_ctx // w, ctx_block, 0)

    nr = NAT_ROWS

    def row_group(gi, carry):
        qrows = pl.ds(pl.multiple_of(n_ctx + gi * (nr * w), nr * w), nr * w)
        qg = q_ref[0, qrows, :]
        lane_g = lax.broadcasted_iota(jnp.int32, qg.shape, 1) >= HEAD_DIM
        qh = (jnp.where(lane_g, jnp.zeros_like(qg), qg), jnp.where(lane_g, qg, jnp.zeros_like(qg)))
        s_ctx = [_dot_nt(qh[hh], kc) for hh in range(2)]
        s_win = {}
        vbs = []
        for rr in range(nr):
            r = gi * nr + rr
            r0 = jnp.clip(r - WIN_R // 2, 0, grid_rows - WIN_R)
            cls = r - r0
            krows = pl.ds(pl.multiple_of(n_ctx + r0 * w, w), band)
            kb = k_ref[0, krows, :]
            vbs.append(v_ref[0, krows, :])
            for hh in range(2):
                s_win[rr, hh] = _dot_nt(qh[hh][rr * w:(rr + 1) * w], kb) + bias_ref[hh, cls]
        p_win = {}
        p_ctx = [[], []]
        den = {}
        for rr in range(nr):
            for hh in range(2):
                sw = s_win[rr, hh]
                sc = s_ctx[hh][rr * w:(rr + 1) * w]
                m = jnp.maximum(jnp.max(sw, axis=-1, keepdims=True),
                                jnp.max(sc, axis=-1, keepdims=True))
                pw = jnp.exp(sw - m)
                pc = jnp.exp(sc - m)
                den[rr, hh] = jnp.sum(pw, axis=-1, keepdims=True) + jnp.sum(pc, axis=-1, keepdims=True)
                p_win[rr, hh] = pw.astype(BF16)
                p_ctx[hh].append(pc.astype(BF16))
        o_ctx = [_dot(jnp.concatenate(p_ctx[hh], axis=0), vc) for hh in range(2)]
        o_win = {(rr, hh): _dot(p_win[rr, hh], vbs[rr]) for rr in range(nr) for hh in range(2)}
        for rr in range(nr):
            outs = [(o_win[rr, hh] + o_ctx[hh][rr * w:(rr + 1) * w]) / den[rr, hh] for hh in range(2)]
            rows = pl.ds(pl.multiple_of(n_ctx + (gi * nr + rr) * w, w), w)
            o_ref[0, rows, :] = jnp.where(upper, outs[1], outs[0]).astype(BF16)
        return carry

    lax.fori_loop(0, grid_rows // nr, row_group, 0)


def _natten(cq, ck, cv, bias, n_ctx):
    b, s, w = cq.shape
    grid_rows = (s - n_ctx) // GRID_W
    assert grid_rows >= WIN_R and grid_rows % NAT_ROWS == 0
    blk = lambda: pl.BlockSpec((1, s, LANES), lambda i, p: (i, 0, p))
    return pl.pallas_call(
        functools.partial(_natten_kernel, n_ctx=n_ctx, grid_rows=grid_rows),
        grid=(b, w // LANES),
        in_specs=[blk(), blk(), blk(),
                  pl.BlockSpec((2, WIN_R, GRID_W, WIN_R * GRID_W), lambda i, p: (p, 0, 0, 0))],
        out_specs=blk(),
        out_shape=jax.ShapeDtypeStruct((b, s, w), BF16),
        compiler_params=_params("parallel", "parallel"),
        name="neighborhood_attention",
    )(cq, ck, cv, bias)


def _merge_ffn_kernel(xc_ref, xl_ref, oaf_ref, oab_ref, og_ref, yb_ref, yc_ref, gt_ref, mod_ref, gn_ref, wbr_ref,
                      wo_ref, nw_ref, wgu_ref, wd_ref, o_ref, *, d, hidden, n_split, tile0,
                      n_ctx_tiles):
    ya = []
    for h in range(A_HEADS):
        sl = slice(h * A_DK, (h + 1) * A_DK)
        oh = oaf_ref[0, :, sl] + oab_ref[0, :, sl]
        yh = oh * lax.rsqrt(jnp.mean(oh * oh, axis=-1, keepdims=True) + EPS) * gn_ref[...]
        ya.append((yh * _silu(og_ref[0, :, sl].astype(F32))).astype(BF16))
    ys = (jnp.concatenate(ya, axis=-1), yb_ref[0], yc_ref[0])
    merged = None
    for k in range(N_BRANCH):
        term = gt_ref[0, :, k * d:(k + 1) * d].astype(F32) * _dot(ys[k], wbr_ref[0, k])
        merged = term if merged is None else merged + term
    g1 = mod_ref[0, 0, :, 2 * d:3 * d]
    x = _stream_tile(xc_ref, xl_ref, pl.program_id(1) + tile0, n_ctx_tiles)
    x = x + g1 * _dot(merged.astype(BF16), wo_ref[0])

    y = x * lax.rsqrt(jnp.mean(x * x, axis=-1, keepdims=True) + EPS) * nw_ref[...]
    sh2 = mod_ref[0, 0, :, 3 * d:4 * d]
    sc2 = mod_ref[0, 0, :, 4 * d:5 * d]
    g2 = mod_ref[0, 0, :, 5 * d:6 * d]
    h = (y * (1.0 + sc2) + sh2).astype(BF16)
    step = hidden // n_split
    acc = None
    for c in range(n_split):
        a = _dot(h, wgu_ref[0, :, c * step:(c + 1) * step])
        g = _dot(h, wgu_ref[0, :, hidden + c * step:hidden + (c + 1) * step])
        part = _dot((_silu(a) * g).astype(BF16), wd_ref[0, c * step:(c + 1) * step, :])
        acc = part if acc is None else acc + part
    o_ref[0] = x + g2 * acc


def _merge_ffn(x_ctx, x_lat, lat_base, oaf, oab, og, yb, yc, gt, modsel, gn, wbr, wo, norm_w, wgu, wd,
               layer, n_ctx_tiles, tile0, n_tiles):
    b, _, d = x_ctx.shape
    tm = TOKEN_TILE
    hidden = wd.shape[1]
    n_split = 2
    assert hidden % (n_split * LANES) == 0
    tok = lambda w: pl.BlockSpec((1, tm, w), lambda i, j: (i, j + tile0, 0))
    return pl.pallas_call(
        functools.partial(_merge_ffn_kernel, d=d, hidden=hidden, n_split=n_split, tile0=tile0,
                          n_ctx_tiles=n_ctx_tiles),
        grid=(b, n_tiles),
        in_specs=[
            *_stream_specs(tm, d, n_ctx_tiles, lat_base, tile0), tok(oaf.shape[-1]), tok(oab.shape[-1]), tok(og.shape[-1]), tok(yb.shape[-1]), tok(yc.shape[-1]),
            tok(gt.shape[-1]),
            pl.BlockSpec((1, 1, 1, modsel.shape[-1]),
                         lambda i, j: (i, (j + tile0 >= n_ctx_tiles).astype(jnp.int32), 0, 0)),
            _resident((1, A_DK), lambda i, j: (0, 0)),
            _resident((1,) + wbr.shape[1:], lambda i, j: (layer, 0, 0, 0)),
            _resident((1,) + wo.shape[1:], lambda i, j: (layer, 0, 0)),
            _resident((1, d), lambda i, j: (0, 0)),
            _resident((1,) + wgu.shape[1:], lambda i, j: (layer, 0, 0)),
            _resident((1,) + wd.shape[1:], lambda i, j: (layer, 0, 0)),
        ],
        out_specs=pl.BlockSpec((1, tm, d), lambda i, j: (i, j, 0)),
        out_shape=jax.ShapeDtypeStruct((b, n_tiles * tm, d), F32),
        compiler_params=_params("parallel", "parallel"),
        name="merge_ffn",
    )(x_ctx, x_lat, oaf, oab, og, yb, yc, gt, modsel, gn, wbr, wo, norm_w, wgu, wd)


def _rope_tables(t, n_ctx):
    pos = jnp.arange(t)
    row = (pos // GRID_W).astype(F32)
    colp = (pos % GRID_W).astype(F32)
    n = HEAD_DIM // 4
    inv = ROPE_THETA ** (-jnp.arange(n, dtype=F32) / n)
    ang = jnp.concatenate([row[:, None] * inv, colp[:, None] * inv], axis=-1)
    cos = jnp.cos(ang)
    sin = jnp.sin(ang)
    reps = LANES // HEAD_DIM
    cos_t = jnp.tile(jnp.concatenate([cos, cos], axis=-1), (1, reps))
    sin_t = jnp.tile(jnp.concatenate([-sin, sin], axis=-1), (1, reps))
    cos_t = jnp.concatenate([jnp.ones((n_ctx, LANES), F32), cos_t], axis=0)
    sin_t = jnp.concatenate([jnp.zeros((n_ctx, LANES), F32), sin_t], axis=0)
    return cos_t, sin_t


def _deinterleave_heads(a):
    lead = a.shape[:-1]
    n = a.shape[-1] // HEAD_DIM
    a = a.reshape(lead + (n, HEAD_DIM // 2, 2))
    return jnp.swapaxes(a, -1, -2).reshape(lead + (n * HEAD_DIM,))


def _inproj_weight(w):
    b0 = 5 * A_HEADS * A_DK
    b1 = b0 + (B_HEADS + B_KV_HEADS) * HEAD_DIM
    return jnp.concatenate([w[..., :b0], _deinterleave_heads(w[..., b0:b1]), w[..., b1:]],
                           axis=-1).astype(BF16)


def kernel(x, c, ctx, c_ctx, w_mod, b_mod, norm_mix, norm_ffn, w_in, lb_raw, gn_a, qn_b, kn_b,
           qn_c, kn_c, rel_bias, w_branch, w_out, w_gate_up, w_down):
    bn, t, d = x.shape
    n_ctx = ctx.shape[1]
    depth = w_mod.shape[0]
    tm = TOKEN_TILE
    assert n_ctx % tm == 0 and t % tm == 0 and t % GRID_W == 0
    n_ctx_tiles = n_ctx // tm
    aw = A_HEADS * A_DK

    c_all = jnp.zeros((8, d), F32).at[:bn].set(c).at[bn].set(c_ctx)
    mod = _modulation(c_all, w_mod, b_mod)
    modsel = jnp.stack([jnp.broadcast_to(mod[:, bn:bn + 1], (depth, bn, 6 * d)), mod[:, :bn]],
                       axis=2)[:, :, :, None, :]

    lbp = jax.nn.softmax(lb_raw.astype(F32), axis=0)
    lb_all = jnp.clip(jnp.cumsum(lbp, axis=0) - lbp[:1], 0.0, 1.0 - 1e-6)
    zeros2 = jnp.zeros_like(lb_all)
    lbc_all = jnp.concatenate([jnp.log(lb_all + LB_TINY), jnp.log1p(-lb_all), 1.0 - lb_all, zeros2],
                              axis=1)

    cos_t, sin_t = _rope_tables(t, n_ctx)
    lane = np.arange(LANES)
    gmat = (lane[:, None] // HEAD_DIM == lane[None, :] // HEAD_DIM) / HEAD_DIM
    gmat = jnp.asarray(np.concatenate([gmat, gmat], axis=0), BF16)
    dmat, smask, qrole = _hgrn_tables()
    tables = (jnp.asarray(np.concatenate([dmat, dmat], axis=2), BF16), jnp.asarray(smask),
              jnp.asarray(qrole))

    x_ctx, x_lat, lat_base = ctx, x, 0
    w_in_bf = _inproj_weight(w_in)
    w_br_bf, w_o_bf = w_branch.astype(BF16), w_out.astype(BF16)
    w_gu_bf, w_d_bf = w_gate_up.astype(BF16), w_down.astype(BF16)
    headw_all = jnp.stack([jnp.tile(_deinterleave_heads(qn_b), (1, B_HEADS)),
                           jnp.tile(_deinterleave_heads(kn_b), (1, B_HEADS)),
                           jnp.tile(qn_c, (1, C_HEADS)), jnp.tile(kn_c, (1, C_HEADS))],
                          axis=1).astype(F32)
    headw_all = jnp.concatenate([headw_all, jnp.zeros_like(headw_all)], axis=1)
    n_heads_c = rel_bias.shape[1]
    bias_all = _natten_bias(rel_bias.reshape((depth * n_heads_c,) + rel_bias.shape[2:]))
    bias_all = bias_all.reshape((depth, n_heads_c) + bias_all.shape[1:])
    for l in range(depth):
        last = l == depth - 1
        (aq, alff, akf, alfb, akb, av, aog, bq, bk, bv, cq, ck, cv, gt) = _in_projection(
            x_ctx, x_lat, lat_base, t + n_ctx, modsel[l], norm_mix[l][None, :], w_in_bf, l,
            lbc_all[l], headw_all[l], cos_t, sin_t, gmat, n_ctx_tiles)
        oaf, oab = _hgrn(aq, alff, akf, alfb, akb, av, tables, n_ctx)
        yb = _gqa(bq, bk, bv, n_ctx)
        yc = _natten(cq, ck, cv, bias_all[l], n_ctx)
        tile0 = n_ctx_tiles if last else 0
        n_tiles = (t if last else t + n_ctx) // tm
        xs = _merge_ffn(x_ctx, x_lat, lat_base, oaf, oab, aog, yb, yc, gt, modsel[l],
                        gn_a[l][None, :], w_br_bf, w_o_bf, norm_ffn[l][None, :], w_gu_bf, w_d_bf,
                        l, n_ctx_tiles, tile0, n_tiles)
        x_ctx, x_lat, lat_base = xs, xs, n_ctx_tiles
    return xs
```

```python
import functools

import numpy as np
import jax
import jax.numpy as jnp
from jax import lax
from jax.experimental import pallas as pl
from jax.experimental.pallas import tpu as pltpu

GRID_W = 64
A_HEADS = 4
A_DK = 128
B_HEADS = 8
B_KV_HEADS = 2
B_GROUP = B_HEADS // B_KV_HEADS
HEAD_DIM = 64
C_HEADS = 8
WIN_R = 8
WIN_C = 16
N_BRANCH = 3
ROPE_THETA = 10000.0
EPS = 1e-6
LB_TINY = 1e-30

LANES = 128
TOKEN_TILE = 256
A_CHUNK = 64
A_LEVELS = 6
A_BATCH = 1
KV_TILE = 512
Q_TILE = 256
NAT_ROWS = 8
NEG_BIG = -1e30
LOG2_E = 1.4426950408889634
BOUND_SLACK = 1.02
MAX_SAFE_SHIFT = 50.0
VMEM_LIMIT = 56 * 1024 * 1024

F32 = jnp.float32
BF16 = jnp.bfloat16


def _dot(a, b):
    return jnp.dot(a, b, preferred_element_type=F32)


def _dot_nt(a, b):
    return lax.dot_general(a, b, (((1,), (1,)), ((), ())), preferred_element_type=F32)


def _dot_tn(a, b):
    return lax.dot_general(a, b, (((0,), (0,)), ((), ())), preferred_element_type=F32)


def _split_bf16(x):
    hi = x.astype(BF16)
    lo = (x - hi.astype(F32)).astype(BF16)
    return hi, lo


def _silu(x):
    return x * jax.nn.sigmoid(x)


def _params(*sem):
    return pltpu.CompilerParams(dimension_semantics=sem, vmem_limit_bytes=VMEM_LIMIT)


def _resident(shape, index_map):
    return pl.BlockSpec(shape, index_map, pipeline_mode=pl.Buffered(1))


def _mod_kernel(c_ref, w_ref, b_ref, o_ref):
    a = _silu(c_ref[...])
    a_hi, a_lo = _split_bf16(a)
    w = w_ref[0]
    w_hi, w_lo = _split_bf16(w)
    acc = _dot(a_hi, w_hi) + _dot(a_hi, w_lo) + _dot(a_lo, w_hi)
    o_ref[0] = acc + b_ref[0]


def _modulation(c_all, w_mod, b_mod):
    depth, d, n = w_mod.shape
    rows = c_all.shape[0]
    tn = 1536
    return pl.pallas_call(
        _mod_kernel,
        grid=(depth, n // tn),
        in_specs=[
            pl.BlockSpec((rows, d), lambda l, j: (0, 0)),
            pl.BlockSpec((1, d, tn), lambda l, j: (l, 0, j)),
            pl.BlockSpec((1, 1, tn), lambda l, j: (l, 0, j)),
        ],
        out_specs=pl.BlockSpec((1, rows, tn), lambda l, j: (l, 0, j)),
        out_shape=jax.ShapeDtypeStruct((depth, rows, n), F32),
        compiler_params=_params("arbitrary", "arbitrary"),
        name="modulation",
    )(c_all, w_mod, b_mod.reshape(depth, 1, n))


def _group_mean_sq(x, gmat):
    hi, lo = _split_bf16(x * x)
    return _dot(jnp.concatenate([hi, lo], axis=-1), gmat)


def _head_norm(x, gmat, w):
    return x * lax.rsqrt(_group_mean_sq(x, gmat) + EPS) * w


def _rope(x, cos, sin_signed, first_half):
    rot = jnp.where(first_half, pltpu.roll(x, LANES - HEAD_DIM // 2, 1),
                    pltpu.roll(x, HEAD_DIM // 2, 1))
    return x * cos + rot * sin_signed


def _log_forget(z, la, l1):
    ls = jnp.minimum(z, 0.0) - jnp.log1p(jnp.exp(-jnp.abs(z)))
    t = l1 + ls
    return jnp.maximum(la, t) + jnp.log1p(jnp.exp(-jnp.abs(la - t)))


def _stream_tile(xc_ref, xl_ref, tile, n_ctx_tiles):
    return jnp.where(tile < n_ctx_tiles, xc_ref[0], xl_ref[0])


def _stream_specs(tm, d, n_ctx_tiles, lat_base, tile0=0):
    ctx = pl.BlockSpec((1, tm, d), lambda i, j: (i, jnp.minimum(j + tile0, n_ctx_tiles - 1), 0))
    lat = pl.BlockSpec((1, tm, d),
                       lambda i, j: (i, jnp.maximum(j + tile0 - n_ctx_tiles, 0) + lat_base, 0))
    return ctx, lat


def _inproj_kernel(xc_ref, xl_ref, mod_ref, nw_ref, w_ref, lbc_ref, hw_ref, cos_ref, sin_ref, gmat_ref,
                   aq_ref, alff_ref, akf_ref, alfb_ref, akb_ref, av_ref, aog_ref,
                   bq_ref, bk_ref, bv_ref, cq_ref, ck_ref, cv_ref, gt_ref, *, d, n_ctx_tiles):
    x = _stream_tile(xc_ref, xl_ref, pl.program_id(1), n_ctx_tiles)
    ms = jnp.mean(x * x, axis=-1, keepdims=True)
    y = x * lax.rsqrt(ms + EPS) * nw_ref[...]
    sh1 = mod_ref[0, 0, :, 0:d]
    sc1 = mod_ref[0, 0, :, d:2 * d]
    u = (y * (1.0 + sc1) + sh1).astype(BF16)

    aw = A_HEADS * A_DK
    col = [0]

    def proj(width):
        c0 = col[0]
        col[0] = c0 + width
        return _dot(u, w_ref[0, :, c0:c0 + width])

    col[0] = w_ref.shape[2] - N_BRANCH * d
    for k in range(N_BRANCH):
        gt_ref[0, :, k * d:(k + 1) * d] = jax.nn.sigmoid(proj(d)).astype(BF16)
    col[0] = 0

    aq_ref[0] = (proj(aw) * (A_DK ** -0.5)).astype(BF16)
    for lf_ref, k_ref, row in ((alff_ref, akf_ref, 0), (alfb_ref, akb_ref, 1)):
        z = proj(aw)
        la = lbc_ref[row:row + 1, :]
        l1 = lbc_ref[2 + row:3 + row, :]
        oml = lbc_ref[4 + row:5 + row, :]
        lf_ref[0] = _log_forget(z, la, l1)
        k_ref[0] = (oml * jax.nn.sigmoid(-z)).astype(BF16)
    av_ref[0] = proj(aw).astype(BF16)
    aog_ref[0] = proj(aw).astype(BF16)

    gmat = gmat_ref[...]
    lane = lax.broadcasted_iota(jnp.int32, (x.shape[0], LANES), 1)
    first_half = (lane % HEAD_DIM) < (HEAD_DIM // 2)
    cos = cos_ref[...]
    sin = sin_ref[...]
    scale = HEAD_DIM ** -0.5

    zq = proj(B_HEADS * HEAD_DIM)
    for j in range(B_HEADS * HEAD_DIM // LANES):
        sl = slice(j * LANES, (j + 1) * LANES)
        qn = _head_norm(zq[:, sl], gmat, hw_ref[0:1, sl])
        bq_ref[0, :, sl] = (_rope(qn, cos, sin, first_half) * (scale * LOG2_E)).astype(BF16)
    zk = proj(B_KV_HEADS * HEAD_DIM)
    kn = _head_norm(zk, gmat, hw_ref[1:2, 0:LANES])
    bk_ref[0] = _rope(kn, cos, sin, first_half).astype(BF16)
    bv_ref[0] = proj(B_KV_HEADS * HEAD_DIM).astype(BF16)

    zq = proj(C_HEADS * HEAD_DIM)
    for j in range(C_HEADS * HEAD_DIM // LANES):
        sl = slice(j * LANES, (j + 1) * LANES)
        cq_ref[0, :, sl] = (_head_norm(zq[:, sl], gmat, hw_ref[2:3, sl]) * scale).astype(BF16)
    zk = proj(C_HEADS * HEAD_DIM)
    for j in range(C_HEADS * HEAD_DIM // LANES):
        sl = slice(j * LANES, (j + 1) * LANES)
        ck_ref[0, :, sl] = _head_norm(zk[:, sl], gmat, hw_ref[3:4, sl]).astype(BF16)
    cv_ref[0] = proj(C_HEADS * HEAD_DIM).astype(BF16)


def _in_projection(x_ctx, x_lat, lat_base, s, modsel, norm_w, w_bf, layer, lbc, headw, cos_t, sin_t,
                   gmat, n_ctx_tiles):
    b, _, d = x_ctx.shape
    tm = TOKEN_TILE
    aw = A_HEADS * A_DK
    n_in = w_bf.shape[2]
    widths = [(aw, BF16), (aw, F32), (aw, BF16), (aw, F32), (aw, BF16), (aw, BF16), (aw, BF16),
              (B_HEADS * HEAD_DIM, BF16), (B_KV_HEADS * HEAD_DIM, BF16), (B_KV_HEADS * HEAD_DIM, BF16),
              (C_HEADS * HEAD_DIM, BF16), (C_HEADS * HEAD_DIM, BF16), (C_HEADS * HEAD_DIM, BF16),
              (N_BRANCH * d, BF16)]
    tok = lambda w: pl.BlockSpec((1, tm, w), lambda i, j: (i, j, 0))
    return pl.pallas_call(
        functools.partial(_inproj_kernel, d=d, n_ctx_tiles=n_ctx_tiles),
        grid=(b, s // tm),
        in_specs=[
            *_stream_specs(tm, d, n_ctx_tiles, lat_base),
            pl.BlockSpec((1, 1, 1, modsel.shape[-1]),
                         lambda i, j: (i, (j >= n_ctx_tiles).astype(jnp.int32), 0, 0)),
            _resident((1, d), lambda i, j: (0, 0)),
            _resident((1, d, n_in), lambda i, j: (layer, 0, 0)),
            _resident(lbc.shape, lambda i, j: (0, 0)),
            _resident(headw.shape, lambda i, j: (0, 0)),
            pl.BlockSpec((tm, LANES), lambda i, j: (j, 0)),
            pl.BlockSpec((tm, LANES), lambda i, j: (j, 0)),
            _resident(gmat.shape, lambda i, j: (0, 0)),
        ],
        out_specs=[tok(w) for w, _ in widths],
        out_shape=[jax.ShapeDtypeStruct((b, s, w), dt) for w, dt in widths],
        compiler_params=_params("parallel", "parallel"),
        name="in_projection",
    )(x_ctx, x_lat, modsel, norm_w, w_bf, lbc, headw, cos_t, sin_t, gmat)


def _hgrn_tables():
    L = A_CHUNK
    ms = [1 << i for i in range(A_LEVELS)]
    dmat = np.zeros((2, (2 + A_LEVELS) * L, L), np.float32)
    smask = np.zeros((2, A_LEVELS + 1, L, L), np.float32)
    qrole = np.zeros((2, A_LEVELS, L, LANES), np.float32)
    t = np.arange(L)
    for dr in range(2):
        fwd = dr == 0
        for r in range(L):
            if fwd:
                dmat[dr, r, :r + 1] = 1.0
                dmat[dr, L + r, r + 1:] = 1.0
            else:
                dmat[dr, r, r:] = 1.0
                dmat[dr, L + r, :r] = 1.0
        for li, m in enumerate(ms):
            base = (2 + li) * L
            for r in range(L):
                start = (r // (2 * m)) * 2 * m
                mid = start + m
                upper = r >= mid
                if fwd:
                    if upper:
                        dmat[dr, base + r, mid:r + 1] = 1.0
                    else:
                        dmat[dr, base + r, r + 1:mid] = 1.0
                else:
                    if not upper:
                        dmat[dr, base + r, r:mid] = 1.0
                    else:
                        dmat[dr, base + r, mid:r] = 1.0
                qrole[dr, li, r, :] = 1.0 if (upper == fwd) else 0.0
            same = (t[:, None] // (2 * m)) == (t[None, :] // (2 * m))
            t_up = (t[:, None] % (2 * m)) >= m
            s_up = (t[None, :] % (2 * m)) >= m
            if fwd:
                smask[dr, li] = same & t_up & ~s_up
            else:
                smask[dr, li] = same & ~t_up & s_up
        smask[dr, A_LEVELS] = np.eye(L)
    return dmat, smask, qrole


def _hgrn_kernel(qf_ref, lff_ref, kf_ref, vf_ref, qb_ref, lfb_ref, kb_ref, vb_ref,
                 dmat_ref, smask_ref, qrole_ref, of_ref, ob_ref, stf_ref, stb_ref, *, n_chunks,
                 n_batch):
    L = A_CHUNK

    @pl.when(pl.program_id(1) == 0)
    def _():
        stf_ref[...] = jnp.zeros_like(stf_ref)
        stb_ref[...] = jnp.zeros_like(stb_ref)

    dirs = ((0, qf_ref, lff_ref, kf_ref, vf_ref, of_ref, stf_ref),
            (1, qb_ref, lfb_ref, kb_ref, vb_ref, ob_ref, stb_ref))

    def body(i, carry):
        chains = []
        for bi in range(n_batch):
            for dr, q_ref, lf_ref, k_ref, v_ref, o_ref, st_ref in dirs:
                c = i if dr == 0 else n_chunks - 1 - i
                rows = pl.ds(pl.multiple_of(c * L, L), L)
                dm = dmat_ref[dr]
                for hp in range(A_HEADS // 2):
                    g_hi, g_lo = _split_bf16(lf_ref[bi, rows, 2 * hp * A_DK:(2 * hp + 2) * A_DK])
                    ex2 = jnp.exp(_dot(dm, jnp.concatenate([g_hi, g_lo], axis=0)))
                    for hh in range(2):
                        h = 2 * hp + hh
                        chains.append((bi, dr, h, rows, ex2[:, hh * A_DK:(hh + 1) * A_DK],
                                       q_ref, k_ref, v_ref, o_ref, st_ref))
        staged = []
        for bi, dr, h, rows, ex, q_ref, k_ref, v_ref, o_ref, st_ref in chains:
            cols = slice(h * A_DK, (h + 1) * A_DK)
            q = q_ref[bi, rows, cols].astype(F32)
            k = k_ref[bi, rows, cols].astype(F32)
            vb = v_ref[bi, rows, cols]
            eb = ex[0:L]
            ebl = eb[L - 1:L] if dr == 0 else eb[0:1]
            st = st_ref[bi * A_HEADS + h]
            o = _dot_nt((q * eb).astype(BF16), st.astype(BF16))
            st_ref[bi * A_HEADS + h] = st * ebl + _dot_tn(vb, (k * ex[L:2 * L]).astype(BF16))
            parts = [jnp.sum(q * k, axis=-1, keepdims=True)]
            for li in range(A_LEVELS):
                role = qrole_ref[dr, li] > 0.5
                xl = (jnp.where(role, q, k) * ex[(2 + li) * L:(3 + li) * L]).astype(BF16)
                parts.append(_dot_nt(xl, xl))
            staged.append((bi, dr, rows, cols, o, parts, vb, o_ref))
        for bi, dr, rows, cols, o, parts, vb, o_ref in staged:
            scores = parts[0] * smask_ref[dr, A_LEVELS]
            for li in range(A_LEVELS):
                scores = scores + parts[1 + li] * smask_ref[dr, li]
            o_ref[bi, rows, cols] = o + _dot(scores.astype(BF16), vb)
        return carry

    lax.fori_loop(0, n_chunks, body, 0)


def _hgrn(aq, alff, akf, alfb, akb, av, tables, n_ctx):
    b, s, w = aq.shape
    ts = TOKEN_TILE
    n_tiles = s // ts
    n_ctx_tiles = n_ctx // ts
    dmat, smask, qrole = tables

    def bwd_tile(j):
        return jnp.where(j < n_ctx_tiles, n_ctx_tiles - 1 - j, n_tiles - 1 - (j - n_ctx_tiles))

    nb = A_BATCH if b % A_BATCH == 0 else 1
    fwd = lambda: pl.BlockSpec((nb, ts, w), lambda i, j: (i, j, 0))
    bwd = lambda: pl.BlockSpec((nb, ts, w), lambda i, j: (i, bwd_tile(j), 0))
    const = lambda a: _resident(a.shape, lambda i, j: (0,) * a.ndim)
    state = pltpu.VMEM((nb * A_HEADS, A_DK, A_DK), F32)
    return pl.pallas_call(
        functools.partial(_hgrn_kernel, n_chunks=ts // A_CHUNK, n_batch=nb),
        grid=(b // nb, n_tiles),
        in_specs=[fwd(), fwd(), fwd(), fwd(), bwd(), bwd(), bwd(), bwd(),
                  const(dmat), const(smask), const(qrole)],
        out_specs=[fwd(), bwd()],
        out_shape=[jax.ShapeDtypeStruct((b, s, w), F32)] * 2,
        scratch_shapes=[state, state],
        compiler_params=_params("parallel", "arbitrary"),
        name="hgrn2_scan",
    )(aq, alff, akf, av, aq, alfb, akb, av, dmat, smask, qrole)


def _gqa_kernel(q_ref, k_ref, v_ref, o_ref, qs_ref, kmax_ref, *, n_ctx, s):
    tq = Q_TILE
    tk = KV_TILE
    j = pl.program_id(1)
    is_latent = j >= n_ctx // tq
    n_lat = (s - n_ctx) // tk
    lane = lax.broadcasted_iota(jnp.int32, (tq, LANES), 1)
    upper = lane >= HEAD_DIM
    ctx_rows = pl.ds(0, n_ctx)

    @pl.when(j == 0)
    def _():
        li = lax.broadcasted_iota(jnp.int32, (LANES, LANES), 0) // HEAD_DIM
        lj = lax.broadcasted_iota(jnp.int32, (LANES, LANES), 1) // HEAD_DIM
        head_sum = jnp.where(li == lj, 1.0, 0.0).astype(BF16)

        def body(c, mx):
            kk = k_ref[0, pl.ds(pl.multiple_of(c * tk, tk), tk), :].astype(F32)
            hi, lo = _split_bf16(kk * kk)
            return jnp.maximum(mx, _dot(hi, head_sum) + _dot(lo, head_sum))

        mx = lax.fori_loop(0, n_lat, body, jnp.zeros((tk, LANES), F32))
        kk = k_ref[0, pl.ds(s - n_ctx, n_ctx), :].astype(F32)
        hi, lo = _split_bf16(kk * kk)
        tail = _dot(hi, head_sum) + _dot(lo, head_sum)
        kmax_ref[...] = jnp.maximum(jnp.max(mx, axis=0, keepdims=True),
                                    jnp.max(tail, axis=0, keepdims=True))

    def lat_rows(c):
        return pl.ds(n_ctx + c * tk, tk)

    def scores(rows):
        return _dot_nt(qs_ref[...], k_ref[0, rows, :])

    def lane_max(sc):
        out = sc[:, 0:LANES]
        for jb in range(1, sc.shape[1] // LANES):
            out = jnp.maximum(out, sc[:, jb * LANES:(jb + 1) * LANES])
        return out

    def row_max(lane_wise):
        return jnp.broadcast_to(jnp.max(lane_wise, axis=-1, keepdims=True), lane_wise.shape)

    def probs(sc, mb):
        ps = []
        ls = None
        for jb in range(sc.shape[1] // LANES):
            pj = jnp.exp2(sc[:, jb * LANES:(jb + 1) * LANES] - mb)
            ls = pj if ls is None else ls + pj
            ps.append(pj.astype(BF16))
        return jnp.concatenate(ps, axis=-1), ls

    def finish(kv, ls, acc):
        out = acc * (1.0 / jnp.sum(ls, axis=-1, keepdims=True))
        pair_out = [None, None]
        for gi in range(B_GROUP):
            h = kv * B_GROUP + gi
            oh = out[gi * tq:(gi + 1) * tq, :]
            if h % 2 != kv:
                oh = pltpu.roll(oh, HEAD_DIM, 1)
            pair_out[h % 2] = oh
            if h % 2 == 1:
                o_ref[0, :, (h // 2) * LANES:(h // 2 + 1) * LANES] = jnp.where(
                    upper, pair_out[1], pair_out[0]).astype(BF16)

    def context_only(kv):
        sc = scores(ctx_rows)
        p, ls = probs(sc, row_max(lane_max(sc)))
        finish(kv, ls, _dot(p, v_ref[0, ctx_rows, :]))

    def exact_row_max():
        def body(c, mx):
            rows = pl.ds(pl.multiple_of(n_ctx + c * tk, LANES), tk)
            return jnp.maximum(mx, lane_max(scores(rows)))

        return row_max(lax.fori_loop(0, n_lat, body, lane_max(scores(ctx_rows))))

    def all_keys(kv, shift_fn):
        mb = shift_fn()
        sc_next = scores(ctx_rows)
        ls = None
        acc = None
        for c in range(-1, n_lat):
            rows = ctx_rows if c < 0 else lat_rows(c)
            sc = sc_next
            if c + 1 < n_lat:
                sc_next = scores(lat_rows(c + 1))
            p, lt = probs(sc, mb)
            pv = _dot(p, v_ref[0, rows, :])
            ls = lt if ls is None else ls + lt
            acc = pv if acc is None else acc + pv
        finish(kv, ls, acc)

    for kv in range(B_KV_HEADS):
        for gi in range(B_GROUP):
            h = kv * B_GROUP + gi
            blk = q_ref[0, :, (h // 2) * LANES:(h // 2 + 1) * LANES].astype(F32)
            if h % 2 != kv:
                blk = pltpu.roll(blk, HEAD_DIM, 1)
            keep = upper if kv == 1 else jnp.logical_not(upper)
            qs_ref[gi * tq:(gi + 1) * tq, :] = jnp.where(keep, blk, 0.0).astype(BF16)
        qsq = qs_ref[...].astype(F32)
        qn2 = jnp.sum(qsq * qsq, axis=-1, keepdims=True)
        kn2 = jnp.max(kmax_ref[:, kv * HEAD_DIM:(kv + 1) * HEAD_DIM], axis=-1, keepdims=True)
        bound = jnp.broadcast_to(jnp.sqrt(qn2 * kn2) * BOUND_SLACK, (B_GROUP * tq, LANES))
        bound_ok = jnp.max(bound) <= MAX_SAFE_SHIFT
        pl.when(jnp.logical_and(is_latent, bound_ok))(
            functools.partial(all_keys, kv, lambda: bound))
        pl.when(jnp.logical_and(is_latent, jnp.logical_not(bound_ok)))(
            functools.partial(all_keys, kv, exact_row_max))
        pl.when(jnp.logical_not(is_latent))(functools.partial(context_only, kv))


def _gqa(bq, bk, bv, n_ctx):
    b, s, w = bq.shape
    tq = Q_TILE
    return pl.pallas_call(
        functools.partial(_gqa_kernel, n_ctx=n_ctx, s=s),
        grid=(b, s // tq),
        in_specs=[
            pl.BlockSpec((1, tq, w), lambda i, j: (i, j, 0)),
            pl.BlockSpec((1, s, LANES), lambda i, j: (i, 0, 0)),
            pl.BlockSpec((1, s, LANES), lambda i, j: (i, 0, 0)),
        ],
        out_specs=pl.BlockSpec((1, tq, w), lambda i, j: (i, j, 0)),
        out_shape=jax.ShapeDtypeStruct((b, s, w), BF16),
        scratch_shapes=[pltpu.VMEM((B_GROUP * tq, LANES), BF16), pltpu.VMEM((1, LANES), F32)],
        compiler_params=_params("parallel", "arbitrary"),
        name="gqa_attention",
    )(bq, bk, bv)


def _natten_bias(rel_bias):
    h = rel_bias.shape[0]
    qcol = np.arange(GRID_W)
    c0 = np.clip(qcol - WIN_C // 2, 0, GRID_W - WIN_C)
    kcol = np.arange(GRID_W)
    valid = (kcol[None, :] >= c0[:, None]) & (kcol[None, :] < c0[:, None] + WIN_C)
    dc = kcol[None, :] - qcol[:, None] + (WIN_C - 1)
    pick = (dc[None, :, :] == np.arange(2 * WIN_C - 1)[:, None, None]) & valid[None]
    cols = jnp.einsum('hdi,iqk->hdqk', rel_bias.astype(F32), jnp.asarray(pick, F32),
                      precision=lax.Precision.HIGHEST)
    cols = jnp.where(valid[None, None], cols, NEG_BIG)
    per_class = [jnp.swapaxes(cols[:, WIN_R - 1 - c:2 * WIN_R - 1 - c], 1, 2) for c in range(WIN_R)]
    return jnp.stack(per_class, axis=1).reshape(h, WIN_R, GRID_W, WIN_R * GRID_W)


def _natten_kernel(q_ref, k_ref, v_ref, bias_ref, o_ref, *, n_ctx, grid_rows):
    w = GRID_W
    band = WIN_R * w
    lane = lax.broadcasted_iota(jnp.int32, (w, LANES), 1)
    upper = lane >= HEAD_DIM
    kc = k_ref[0, 0:n_ctx, :]
    vc = v_ref[0, 0:n_ctx, :]

    def heads(q, fn):
        outs = []
        for hh in range(2):
            keep = upper if hh == 1 else jnp.logical_not(upper)
            outs.append(fn(hh, jnp.where(keep, q, jnp.zeros_like(q))))
        return jnp.where(upper, outs[1], outs[0])

    def ctx_block(i, carry):
        rows = pl.ds(pl.multiple_of(i * w, w), w)

        def attend(hh, qh):
            sc = _dot_nt(qh, kc)
            p = jnp.exp(sc - jnp.max(sc, axis=-1, keepdims=True))
            return _dot(p.astype(BF16), vc) / jnp.sum(p, axis=-1, keepdims=True)

        o_ref[0, rows, :] = heads(q_ref[0, rows, :], attend).astype(BF16)
        return carry

    lax.fori_loop(0, n_ctx // w, ctx_block, 0)

    nr = NAT_ROWS

    def row_group(gi, carry):
        qrows = pl.ds(pl.multiple_of(n_ctx + gi * (nr * w), nr * w), nr * w)
        qg = q_ref[0, qrows, :]
        lane_g = lax.broadcasted_iota(jnp.int32, qg.shape, 1) >= HEAD_DIM
        qh = (jnp.where(lane_g, jnp.zeros_like(qg), qg), jnp.where(lane_g, qg, jnp.zeros_like(qg)))
        s_ctx = [_dot_nt(qh[hh], kc) for hh in range(2)]
        s_win = {}
        vbs = []
        for rr in range(nr):
            r = gi * nr + rr
            r0 = jnp.clip(r - WIN_R // 2, 0, grid_rows - WIN_R)
            cls = r - r0
            krows = pl.ds(pl.multiple_of(n_ctx + r0 * w, w), band)
            kb = k_ref[0, krows, :]
            vbs.append(v_ref[0, krows, :])
            for hh in range(2):
                s_win[rr, hh] = _dot_nt(qh[hh][rr * w:(rr + 1) * w], kb) + bias_ref[hh, cls]
        p_win = {}
        p_ctx = [[], []]
        den = {}
        for rr in range(nr):
            for hh in range(2):
                sw = s_win[rr, hh]
                sc = s_ctx[hh][rr * w:(rr + 1) * w]
                m = jnp.maximum(jnp.max(sw, axis=-1, keepdims=True),
                                jnp.max(sc, axis=-1, keepdims=True))
                pw = jnp.exp(sw - m)
                pc = jnp.exp(sc - m)
                den[rr, hh] = jnp.sum(pw, axis=-1, keepdims=True) + jnp.sum(pc, axis=-1, keepdims=True)
                p_win[rr, hh] = pw.astype(BF16)
                p_ctx[hh].append(pc.astype(BF16))
        o_ctx = [_dot(jnp.concatenate(p_ctx[hh], axis=0), vc) for hh in range(2)]
        o_win = {(rr, hh): _dot(p_win[rr, hh], vbs[rr]) for rr in range(nr) for hh in range(2)}
        for rr in range(nr):
            outs = [(o_win[rr, hh] + o_ctx[hh][rr * w:(rr + 1) * w]) / den[rr, hh] for hh in range(2)]
            rows = pl.ds(pl.multiple_of(n_ctx + (gi * nr + rr) * w, w), w)
            o_ref[0, rows, :] = jnp.where(upper, outs[1], outs[0]).astype(BF16)
        return carry

    lax.fori_loop(0, grid_rows // nr, row_group, 0)


def _natten(cq, ck, cv, bias, layer, n_ctx):
    b, s, w = cq.shape
    pairs = w // LANES
    grid_rows = (s - n_ctx) // GRID_W
    assert grid_rows >= WIN_R and grid_rows % NAT_ROWS == 0
    blk = lambda: pl.BlockSpec((1, s, LANES), lambda i, p: (i, 0, p))
    return pl.pallas_call(
        functools.partial(_natten_kernel, n_ctx=n_ctx, grid_rows=grid_rows),
        grid=(b, w // LANES),
        in_specs=[blk(), blk(), blk(),
                  pl.BlockSpec((2, WIN_R, GRID_W, WIN_R * GRID_W),
                               lambda i, p: (layer * pairs + p, 0, 0, 0))],
        out_specs=blk(),
        out_shape=jax.ShapeDtypeStruct((b, s, w), BF16),
        compiler_params=_params("parallel", "parallel"),
        name="neighborhood_attention",
    )(cq, ck, cv, bias)


def _merge_ffn_kernel(xc_ref, xl_ref, oaf_ref, oab_ref, og_ref, yb_ref, yc_ref, gt_ref, mod_ref, gn_ref, wbr_ref,
                      wo_ref, nw_ref, wgu_ref, wd_ref, o_ref, *, d, hidden, n_split, tile0,
                      n_ctx_tiles):
    ya = []
    for h in range(A_HEADS):
        sl = slice(h * A_DK, (h + 1) * A_DK)
        oh = oaf_ref[0, :, sl] + oab_ref[0, :, sl]
        yh = oh * lax.rsqrt(jnp.mean(oh * oh, axis=-1, keepdims=True) + EPS) * gn_ref[...]
        ya.append((yh * _silu(og_ref[0, :, sl].astype(F32))).astype(BF16))
    ys = (jnp.concatenate(ya, axis=-1), yb_ref[0], yc_ref[0])
    merged = None
    for k in range(N_BRANCH):
        term = gt_ref[0, :, k * d:(k + 1) * d].astype(F32) * _dot(ys[k], wbr_ref[0, k])
        merged = term if merged is None else merged + term
    g1 = mod_ref[0, 0, :, 2 * d:3 * d]
    x = _stream_tile(xc_ref, xl_ref, pl.program_id(1) + tile0, n_ctx_tiles)
    x = x + g1 * _dot(merged.astype(BF16), wo_ref[0])

    y = x * lax.rsqrt(jnp.mean(x * x, axis=-1, keepdims=True) + EPS) * nw_ref[...]
    sh2 = mod_ref[0, 0, :, 3 * d:4 * d]
    sc2 = mod_ref[0, 0, :, 4 * d:5 * d]
    g2 = mod_ref[0, 0, :, 5 * d:6 * d]
    h = (y * (1.0 + sc2) + sh2).astype(BF16)
    step = hidden // n_split
    acc = None
    for c in range(n_split):
        a = _dot(h, wgu_ref[0, :, c * step:(c + 1) * step])
        g = _dot(h, wgu_ref[0, :, hidden + c * step:hidden + (c + 1) * step])
        part = _dot((_silu(a) * g).astype(BF16), wd_ref[0, c * step:(c + 1) * step, :])
        acc = part if acc is None else acc + part
    o_ref[0] = x + g2 * acc


def _merge_ffn(x_ctx, x_lat, lat_base, oaf, oab, og, yb, yc, gt, modsel, gn, wbr, wo, norm_w, wgu, wd,
               layer, n_ctx_tiles, tile0, n_tiles):
    b, _, d = x_ctx.shape
    tm = TOKEN_TILE
    hidden = wd.shape[1]
    n_split = 2
    assert hidden % (n_split * LANES) == 0
    tok = lambda w: pl.BlockSpec((1, tm, w), lambda i, j: (i, j + tile0, 0))
    return pl.pallas_call(
        functools.partial(_merge_ffn_kernel, d=d, hidden=hidden, n_split=n_split, tile0=tile0,
                          n_ctx_tiles=n_ctx_tiles),
        grid=(b, n_tiles),
        in_specs=[
            *_stream_specs(tm, d, n_ctx_tiles, lat_base, tile0), tok(oaf.shape[-1]), tok(oab.shape[-1]), tok(og.shape[-1]), tok(yb.shape[-1]), tok(yc.shape[-1]),
            tok(gt.shape[-1]),
            pl.BlockSpec((1, 1, 1, modsel.shape[-1]),
                         lambda i, j: (i, (j + tile0 >= n_ctx_tiles).astype(jnp.int32), 0, 0)),
            _resident((1, A_DK), lambda i, j: (0, 0)),
            _resident((1,) + wbr.shape[1:], lambda i, j: (layer, 0, 0, 0)),
            _resident((1,) + wo.shape[1:], lambda i, j: (layer, 0, 0)),
            _resident((1, d), lambda i, j: (0, 0)),
            _resident((1,) + wgu.shape[1:], lambda i, j: (layer, 0, 0)),
            _resident((1,) + wd.shape[1:], lambda i, j: (layer, 0, 0)),
        ],
        out_specs=pl.BlockSpec((1, tm, d), lambda i, j: (i, j, 0)),
        out_shape=jax.ShapeDtypeStruct((b, n_tiles * tm, d), F32),
        compiler_params=_params("parallel", "parallel"),
        name="merge_ffn",
    )(x_ctx, x_lat, oaf, oab, og, yb, yc, gt, modsel, gn, wbr, wo, norm_w, wgu, wd)


def _rope_tables(t, n_ctx):
    pos = jnp.arange(t)
    row = (pos // GRID_W).astype(F32)
    colp = (pos % GRID_W).astype(F32)
    n = HEAD_DIM // 4
    inv = ROPE_THETA ** (-jnp.arange(n, dtype=F32) / n)
    ang = jnp.concatenate([row[:, None] * inv, colp[:, None] * inv], axis=-1)
    cos = jnp.cos(ang)
    sin = jnp.sin(ang)
    reps = LANES // HEAD_DIM
    cos_t = jnp.tile(jnp.concatenate([cos, cos], axis=-1), (1, reps))
    sin_t = jnp.tile(jnp.concatenate([-sin, sin], axis=-1), (1, reps))
    cos_t = jnp.concatenate([jnp.ones((n_ctx, LANES), F32), cos_t], axis=0)
    sin_t = jnp.concatenate([jnp.zeros((n_ctx, LANES), F32), sin_t], axis=0)
    return cos_t, sin_t


def _deinterleave_heads(a):
    lead = a.shape[:-1]
    n = a.shape[-1] // HEAD_DIM
    a = a.reshape(lead + (n, HEAD_DIM // 2, 2))
    return jnp.swapaxes(a, -1, -2).reshape(lead + (n * HEAD_DIM,))


def _inproj_weight(w):
    b0 = 5 * A_HEADS * A_DK
    b1 = b0 + (B_HEADS + B_KV_HEADS) * HEAD_DIM
    return w.astype(BF16).at[..., b0:b1].set(_deinterleave_heads(w[..., b0:b1]).astype(BF16))


def kernel(x, c, ctx, c_ctx, w_mod, b_mod, norm_mix, norm_ffn, w_in, lb_raw, gn_a, qn_b, kn_b,
           qn_c, kn_c, rel_bias, w_branch, w_out, w_gate_up, w_down):
    bn, t, d = x.shape
    n_ctx = ctx.shape[1]
    depth = w_mod.shape[0]
    tm = TOKEN_TILE
    assert n_ctx % tm == 0 and t % tm == 0 and t % GRID_W == 0
    n_ctx_tiles = n_ctx // tm
    aw = A_HEADS * A_DK

    c_all = jnp.zeros((8, d), F32).at[:bn].set(c).at[bn].set(c_ctx)
    mod = _modulation(c_all, w_mod, b_mod)
    modsel = jnp.stack([jnp.broadcast_to(mod[:, bn:bn + 1], (depth, bn, 6 * d)), mod[:, :bn]],
                       axis=2)[:, :, :, None, :]

    lbp = jax.nn.softmax(lb_raw.astype(F32), axis=0)
    lb_all = jnp.clip(jnp.cumsum(lbp, axis=0) - lbp[:1], 0.0, 1.0 - 1e-6)
    zeros2 = jnp.zeros_like(lb_all)
    lbc_all = jnp.concatenate([jnp.log(lb_all + LB_TINY), jnp.log1p(-lb_all), 1.0 - lb_all, zeros2],
                              axis=1)

    cos_t, sin_t = _rope_tables(t, n_ctx)
    lane = np.arange(LANES)
    gmat = (lane[:, None] // HEAD_DIM == lane[None, :] // HEAD_DIM) / HEAD_DIM
    gmat = jnp.asarray(np.concatenate([gmat, gmat], axis=0), BF16)
    dmat, smask, qrole = _hgrn_tables()
    tables = (jnp.asarray(np.concatenate([dmat, dmat], axis=2), BF16), jnp.asarray(smask),
              jnp.asarray(qrole))

    x_ctx, x_lat, lat_base = ctx, x, 0
    w_in_bf = _inproj_weight(w_in)
    w_br_bf, w_o_bf = w_branch.astype(BF16), w_out.astype(BF16)
    w_gu_bf, w_d_bf = w_gate_up.astype(BF16), w_down.astype(BF16)
    headw_all = jnp.stack([jnp.tile(_deinterleave_heads(qn_b), (1, B_HEADS)),
                           jnp.tile(_deinterleave_heads(kn_b), (1, B_HEADS)),
                           jnp.tile(qn_c, (1, C_HEADS)), jnp.tile(kn_c, (1, C_HEADS))],
                          axis=1).astype(F32)
    headw_all = jnp.concatenate([headw_all, jnp.zeros_like(headw_all)], axis=1)
    n_heads_c = rel_bias.shape[1]
    bias_all = _natten_bias(rel_bias.reshape((depth * n_heads_c,) + rel_bias.shape[2:]))
    for l in range(depth):
        last = l == depth - 1
        (aq, alff, akf, alfb, akb, av, aog, bq, bk, bv, cq, ck, cv, gt) = _in_projection(
            x_ctx, x_lat, lat_base, t + n_ctx, modsel[l], norm_mix[l][None, :], w_in_bf, l,
            lbc_all[l], headw_all[l], cos_t, sin_t, gmat, n_ctx_tiles)
        oaf, oab = _hgrn(aq, alff, akf, alfb, akb, av, tables, n_ctx)
        yb = _gqa(bq, bk, bv, n_ctx)
        yc = _natten(cq, ck, cv, bias_all, l, n_ctx)
        tile0 = n_ctx_tiles if last else 0
        n_tiles = (t if last else t + n_ctx) // tm
        xs = _merge_ffn(x_ctx, x_lat, lat_base, oaf, oab, aog, yb, yc, gt, modsel[l],
                        gn_a[l][None, :], w_br_bf, w_o_bf, norm_ffn[l][None, :], w_gu_bf, w_d_bf,
                        l, n_ctx_tiles, tile0, n_tiles)
        x_ctx, x_lat, lat_base = xs, xs, n_ctx_tiles
    return xs
```

```python
import functools

import numpy as np
import jax
import jax.numpy as jnp
from jax import lax
from jax.experimental import pallas as pl
from jax.experimental.pallas import tpu as pltpu

GRID_W = 64
A_HEADS = 4
A_DK = 128
B_HEADS = 8
B_KV_HEADS = 2
B_GROUP = B_HEADS // B_KV_HEADS
HEAD_DIM = 64
C_HEADS = 8
WIN_R = 8
WIN_C = 16
N_BRANCH = 3
ROPE_THETA = 10000.0
EPS = 1e-6
LB_TINY = 1e-30

LANES = 128
TOKEN_TILE = 256
A_CHUNK = 64
A_LEVELS = 6
A_BATCH = 1
KV_TILE = 512
Q_TILE = 256
NAT_ROWS = 8
NEG_BIG = -1e30
LOG2_E = 1.4426950408889634
BOUND_SLACK = 1.02
MAX_SAFE_SHIFT = 50.0
VMEM_LIMIT = 56 * 1024 * 1024

F32 = jnp.float32
BF16 = jnp.bfloat16


def _dot(a, b):
    return jnp.dot(a, b, preferred_element_type=F32)


def _dot_nt(a, b):
    return lax.dot_general(a, b, (((1,), (1,)), ((), ())), preferred_element_type=F32)


def _dot_tn(a, b):
    return lax.dot_general(a, b, (((0,), (0,)), ((), ())), preferred_element_type=F32)


def _split_bf16(x):
    hi = x.astype(BF16)
    lo = (x - hi.astype(F32)).astype(BF16)
    return hi, lo


def _silu(x):
    return x * jax.nn.sigmoid(x)


def _params(*sem):
    return pltpu.CompilerParams(dimension_semantics=sem, vmem_limit_bytes=VMEM_LIMIT)


def _resident(shape, index_map):
    return pl.BlockSpec(shape, index_map, pipeline_mode=pl.Buffered(1))


def _mod_kernel(c_ref, w_ref, b_ref, o_ref):
    a = _silu(c_ref[...])
    a_hi, a_lo = _split_bf16(a)
    w = w_ref[0]
    w_hi, w_lo = _split_bf16(w)
    acc = _dot(a_hi, w_hi) + _dot(a_hi, w_lo) + _dot(a_lo, w_hi)
    o_ref[0] = acc + b_ref[0]


def _modulation(c_all, w_mod, b_mod):
    depth, d, n = w_mod.shape
    rows = c_all.shape[0]
    tn = 1536
    return pl.pallas_call(
        _mod_kernel,
        grid=(depth, n // tn),
        in_specs=[
            pl.BlockSpec((rows, d), lambda l, j: (0, 0)),
            pl.BlockSpec((1, d, tn), lambda l, j: (l, 0, j)),
            pl.BlockSpec((1, 1, tn), lambda l, j: (l, 0, j)),
        ],
        out_specs=pl.BlockSpec((1, rows, tn), lambda l, j: (l, 0, j)),
        out_shape=jax.ShapeDtypeStruct((depth, rows, n), F32),
        compiler_params=_params("arbitrary", "arbitrary"),
        name="modulation",
    )(c_all, w_mod, b_mod.reshape(depth, 1, n))


def _group_mean_sq(x, gmat):
    hi, lo = _split_bf16(x * x)
    return _dot(jnp.concatenate([hi, lo], axis=-1), gmat)


def _head_norm(x, gmat, w):
    return x * lax.rsqrt(_group_mean_sq(x, gmat) + EPS) * w


def _rope(x, cos, sin_signed, first_half):
    rot = jnp.where(first_half, pltpu.roll(x, LANES - HEAD_DIM // 2, 1),
                    pltpu.roll(x, HEAD_DIM // 2, 1))
    return x * cos + rot * sin_signed


def _log_forget(z, la, l1):
    ls = jnp.minimum(z, 0.0) - jnp.log1p(jnp.exp(-jnp.abs(z)))
    t = l1 + ls
    return jnp.maximum(la, t) + jnp.log1p(jnp.exp(-jnp.abs(la - t)))


def _stream_tile(xc_ref, xl_ref, tile, n_ctx_tiles):
    return jnp.where(tile < n_ctx_tiles, xc_ref[0], xl_ref[0])


def _stream_specs(tm, d, n_ctx_tiles, lat_base, tile0=0):
    ctx = pl.BlockSpec((1, tm, d), lambda i, j: (i, jnp.minimum(j + tile0, n_ctx_tiles - 1), 0))
    lat = pl.BlockSpec((1, tm, d),
                       lambda i, j: (i, jnp.maximum(j + tile0 - n_ctx_tiles, 0) + lat_base, 0))
    return ctx, lat


def _inproj_kernel(xc_ref, xl_ref, mod_ref, nw_ref, w_ref, lbc_ref, hw_ref, cos_ref, sin_ref, gmat_ref,
                   aq_ref, alff_ref, akf_ref, alfb_ref, akb_ref, av_ref, aog_ref,
                   bq_ref, bk_ref, bv_ref, cq_ref, ck_ref, cv_ref, gt_ref, *, d, n_ctx_tiles):
    x = _stream_tile(xc_ref, xl_ref, pl.program_id(1), n_ctx_tiles)
    ms = jnp.mean(x * x, axis=-1, keepdims=True)
    y = x * lax.rsqrt(ms + EPS) * nw_ref[...]
    sh1 = mod_ref[0, 0, :, 0:d]
    sc1 = mod_ref[0, 0, :, d:2 * d]
    u = (y * (1.0 + sc1) + sh1).astype(BF16)

    aw = A_HEADS * A_DK
    col = [0]

    def proj(width):
        c0 = col[0]
        col[0] = c0 + width
        return _dot(u, w_ref[0, :, c0:c0 + width])

    col[0] = w_ref.shape[2] - N_BRANCH * d
    for k in range(N_BRANCH):
        gt_ref[0, :, k * d:(k + 1) * d] = jax.nn.sigmoid(proj(d)).astype(BF16)
    col[0] = 0

    aq_ref[0] = (proj(aw) * (A_DK ** -0.5)).astype(BF16)
    for lf_ref, k_ref, row in ((alff_ref, akf_ref, 0), (alfb_ref, akb_ref, 1)):
        z = proj(aw)
        la = lbc_ref[row:row + 1, :]
        l1 = lbc_ref[2 + row:3 + row, :]
        oml = lbc_ref[4 + row:5 + row, :]
        lf_ref[0] = _log_forget(z, la, l1)
        k_ref[0] = (oml * jax.nn.sigmoid(-z)).astype(BF16)
    av_ref[0] = proj(aw).astype(BF16)
    aog_ref[0] = proj(aw).astype(BF16)

    gmat = gmat_ref[...]
    lane = lax.broadcasted_iota(jnp.int32, (x.shape[0], LANES), 1)
    first_half = (lane % HEAD_DIM) < (HEAD_DIM // 2)
    cos = cos_ref[...]
    sin = sin_ref[...]
    scale = HEAD_DIM ** -0.5

    zq = proj(B_HEADS * HEAD_DIM)
    for j in range(B_HEADS * HEAD_DIM // LANES):
        sl = slice(j * LANES, (j + 1) * LANES)
        qn = _head_norm(zq[:, sl], gmat, hw_ref[0:1, sl])
        bq_ref[0, :, sl] = (_rope(qn, cos, sin, first_half) * (scale * LOG2_E)).astype(BF16)
    zk = proj(B_KV_HEADS * HEAD_DIM)
    kn = _head_norm(zk, gmat, hw_ref[1:2, 0:LANES])
    bk_ref[0] = _rope(kn, cos, sin, first_half).astype(BF16)
    bv_ref[0] = proj(B_KV_HEADS * HEAD_DIM).astype(BF16)

    zq = proj(C_HEADS * HEAD_DIM)
    for j in range(C_HEADS * HEAD_DIM // LANES):
        sl = slice(j * LANES, (j + 1) * LANES)
        cq_ref[0, :, sl] = (_head_norm(zq[:, sl], gmat, hw_ref[2:3, sl]) * scale).astype(BF16)
    zk = proj(C_HEADS * HEAD_DIM)
    for j in range(C_HEADS * HEAD_DIM // LANES):
        sl = slice(j * LANES, (j + 1) * LANES)
        ck_ref[0, :, sl] = _head_norm(zk[:, sl], gmat, hw_ref[3:4, sl]).astype(BF16)
    cv_ref[0] = proj(C_HEADS * HEAD_DIM).astype(BF16)


def _in_projection(x_ctx, x_lat, lat_base, s, modsel, norm_w, w_bf, layer, lbc, headw, cos_t, sin_t,
                   gmat, n_ctx_tiles):
    b, _, d = x_ctx.shape
    tm = TOKEN_TILE
    aw = A_HEADS * A_DK
    n_in = w_bf.shape[2]
    widths = [(aw, BF16), (aw, F32), (aw, BF16), (aw, F32), (aw, BF16), (aw, BF16), (aw, BF16),
              (B_HEADS * HEAD_DIM, BF16), (B_KV_HEADS * HEAD_DIM, BF16), (B_KV_HEADS * HEAD_DIM, BF16),
              (C_HEADS * HEAD_DIM, BF16), (C_HEADS * HEAD_DIM, BF16), (C_HEADS * HEAD_DIM, BF16),
              (N_BRANCH * d, BF16)]
    tok = lambda w: pl.BlockSpec((1, tm, w), lambda i, j: (i, j, 0))
    return pl.pallas_call(
        functools.partial(_inproj_kernel, d=d, n_ctx_tiles=n_ctx_tiles),
        grid=(b, s // tm),
        in_specs=[
            *_stream_specs(tm, d, n_ctx_tiles, lat_base),
            pl.BlockSpec((1, 1, 1, modsel.shape[-1]),
                         lambda i, j: (i, (j >= n_ctx_tiles).astype(jnp.int32), 0, 0)),
            _resident((1, d), lambda i, j: (0, 0)),
            _resident((1, d, n_in), lambda i, j: (layer, 0, 0)),
            _resident(lbc.shape, lambda i, j: (0, 0)),
            _resident(headw.shape, lambda i, j: (0, 0)),
            pl.BlockSpec((tm, LANES), lambda i, j: (j, 0)),
            pl.BlockSpec((tm, LANES), lambda i, j: (j, 0)),
            _resident(gmat.shape, lambda i, j: (0, 0)),
        ],
        out_specs=[tok(w) for w, _ in widths],
        out_shape=[jax.ShapeDtypeStruct((b, s, w), dt) for w, dt in widths],
        compiler_params=_params("parallel", "parallel"),
        name="in_projection",
    )(x_ctx, x_lat, modsel, norm_w, w_bf, lbc, headw, cos_t, sin_t, gmat)


def _hgrn_tables():
    L = A_CHUNK
    ms = [1 << i for i in range(A_LEVELS)]
    dmat = np.zeros((2, (2 + A_LEVELS) * L, L), np.float32)
    smask = np.zeros((2, A_LEVELS + 1, L, L), np.float32)
    qrole = np.zeros((2, A_LEVELS, L, LANES), np.float32)
    t = np.arange(L)
    for dr in range(2):
        fwd = dr == 0
        for r in range(L):
            if fwd:
                dmat[dr, r, :r + 1] = 1.0
                dmat[dr, L + r, r + 1:] = 1.0
            else:
                dmat[dr, r, r:] = 1.0
                dmat[dr, L + r, :r] = 1.0
        for li, m in enumerate(ms):
            base = (2 + li) * L
            for r in range(L):
                start = (r // (2 * m)) * 2 * m
                mid = start + m
                upper = r >= mid
                if fwd:
                    if upper:
                        dmat[dr, base + r, mid:r + 1] = 1.0
                    else:
                        dmat[dr, base + r, r + 1:mid] = 1.0
                else:
                    if not upper:
                        dmat[dr, base + r, r:mid] = 1.0
                    else:
                        dmat[dr, base + r, mid:r] = 1.0
                qrole[dr, li, r, :] = 1.0 if (upper == fwd) else 0.0
            same = (t[:, None] // (2 * m)) == (t[None, :] // (2 * m))
            t_up = (t[:, None] % (2 * m)) >= m
            s_up = (t[None, :] % (2 * m)) >= m
            if fwd:
                smask[dr, li] = same & t_up & ~s_up
            else:
                smask[dr, li] = same & ~t_up & s_up
        smask[dr, A_LEVELS] = np.eye(L)
    return dmat, smask, qrole


def _hgrn_kernel(qf_ref, lff_ref, kf_ref, vf_ref, qb_ref, lfb_ref, kb_ref, vb_ref,
                 dmat_ref, smask_ref, qrole_ref, of_ref, ob_ref, stf_ref, stb_ref, *, n_chunks,
                 n_batch):
    L = A_CHUNK

    @pl.when(pl.program_id(1) == 0)
    def _():
        stf_ref[...] = jnp.zeros_like(stf_ref)
        stb_ref[...] = jnp.zeros_like(stb_ref)

    dirs = ((0, qf_ref, lff_ref, kf_ref, vf_ref, of_ref, stf_ref),
            (1, qb_ref, lfb_ref, kb_ref, vb_ref, ob_ref, stb_ref))

    def body(i, carry):
        chains = []
        for bi in range(n_batch):
            for dr, q_ref, lf_ref, k_ref, v_ref, o_ref, st_ref in dirs:
                c = i if dr == 0 else n_chunks - 1 - i
                rows = pl.ds(pl.multiple_of(c * L, L), L)
                dm = dmat_ref[dr]
                for hp in range(A_HEADS // 2):
                    g_hi, g_lo = _split_bf16(lf_ref[bi, rows, 2 * hp * A_DK:(2 * hp + 2) * A_DK])
                    ex2 = jnp.exp(_dot(dm, jnp.concatenate([g_hi, g_lo], axis=0)))
                    for hh in range(2):
                        h = 2 * hp + hh
                        chains.append((bi, dr, h, rows, ex2[:, hh * A_DK:(hh + 1) * A_DK],
                                       q_ref, k_ref, v_ref, o_ref, st_ref))
        staged = []
        for bi, dr, h, rows, ex, q_ref, k_ref, v_ref, o_ref, st_ref in chains:
            cols = slice(h * A_DK, (h + 1) * A_DK)
            q = q_ref[bi, rows, cols].astype(F32)
            k = k_ref[bi, rows, cols].astype(F32)
            vb = v_ref[bi, rows, cols]
            eb = ex[0:L]
            ebl = eb[L - 1:L] if dr == 0 else eb[0:1]
            st = st_ref[bi * A_HEADS + h]
            o = _dot_nt((q * eb).astype(BF16), st.astype(BF16))
            st_ref[bi * A_HEADS + h] = st * ebl + _dot_tn(vb, (k * ex[L:2 * L]).astype(BF16))
            parts = [jnp.sum(q * k, axis=-1, keepdims=True)]
            for li in range(A_LEVELS):
                role = qrole_ref[dr, li] > 0.5
                xl = (jnp.where(role, q, k) * ex[(2 + li) * L:(3 + li) * L]).astype(BF16)
                parts.append(_dot_nt(xl, xl))
            staged.append((bi, dr, rows, cols, o, parts, vb, o_ref))
        for bi, dr, rows, cols, o, parts, vb, o_ref in staged:
            scores = parts[0] * smask_ref[dr, A_LEVELS]
            for li in range(A_LEVELS):
                scores = scores + parts[1 + li] * smask_ref[dr, li]
            o_ref[bi, rows, cols] = o + _dot(scores.astype(BF16), vb)
        return carry

    lax.fori_loop(0, n_chunks, body, 0)


def _hgrn(aq, alff, akf, alfb, akb, av, tables, n_ctx):
    b, s, w = aq.shape
    ts = TOKEN_TILE
    n_tiles = s // ts
    n_ctx_tiles = n_ctx // ts
    dmat, smask, qrole = tables

    def bwd_tile(j):
        return jnp.where(j < n_ctx_tiles, n_ctx_tiles - 1 - j, n_tiles - 1 - (j - n_ctx_tiles))

    nb = A_BATCH if b % A_BATCH == 0 else 1
    fwd = lambda: pl.BlockSpec((nb, ts, w), lambda i, j: (i, j, 0))
    bwd = lambda: pl.BlockSpec((nb, ts, w), lambda i, j: (i, bwd_tile(j), 0))
    const = lambda a: _resident(a.shape, lambda i, j: (0,) * a.ndim)
    state = pltpu.VMEM((nb * A_HEADS, A_DK, A_DK), F32)
    return pl.pallas_call(
        functools.partial(_hgrn_kernel, n_chunks=ts // A_CHUNK, n_batch=nb),
        grid=(b // nb, n_tiles),
        in_specs=[fwd(), fwd(), fwd(), fwd(), bwd(), bwd(), bwd(), bwd(),
                  const(dmat), const(smask), const(qrole)],
        out_specs=[fwd(), bwd()],
        out_shape=[jax.ShapeDtypeStruct((b, s, w), F32)] * 2,
        scratch_shapes=[state, state],
        compiler_params=_params("parallel", "arbitrary"),
        name="hgrn2_scan",
    )(aq, alff, akf, av, aq, alfb, akb, av, dmat, smask, qrole)


def _gqa_kernel(q_ref, k_ref, v_ref, o_ref, qs_ref, kmax_ref, *, n_ctx, s):
    tq = Q_TILE
    tk = KV_TILE
    j = pl.program_id(1)
    is_latent = j >= n_ctx // tq
    n_lat = (s - n_ctx) // tk
    lane = lax.broadcasted_iota(jnp.int32, (tq, LANES), 1)
    upper = lane >= HEAD_DIM
    ctx_rows = pl.ds(0, n_ctx)

    @pl.when(j == 0)
    def _():
        li = lax.broadcasted_iota(jnp.int32, (LANES, LANES), 0) // HEAD_DIM
        lj = lax.broadcasted_iota(jnp.int32, (LANES, LANES), 1) // HEAD_DIM
        head_sum = jnp.where(li == lj, 1.0, 0.0).astype(BF16)

        def body(c, mx):
            kk = k_ref[0, pl.ds(pl.multiple_of(c * tk, tk), tk), :].astype(F32)
            hi, lo = _split_bf16(kk * kk)
            return jnp.maximum(mx, _dot(hi, head_sum) + _dot(lo, head_sum))

        mx = lax.fori_loop(0, n_lat, body, jnp.zeros((tk, LANES), F32))
        kk = k_ref[0, pl.ds(s - n_ctx, n_ctx), :].astype(F32)
        hi, lo = _split_bf16(kk * kk)
        tail = _dot(hi, head_sum) + _dot(lo, head_sum)
        kmax_ref[...] = jnp.maximum(jnp.max(mx, axis=0, keepdims=True),
                                    jnp.max(tail, axis=0, keepdims=True))

    def lat_rows(c):
        return pl.ds(n_ctx + c * tk, tk)

    def scores(rows):
        return _dot_nt(qs_ref[...], k_ref[0, rows, :])

    def lane_max(sc):
        out = sc[:, 0:LANES]
        for jb in range(1, sc.shape[1] // LANES):
            out = jnp.maximum(out, sc[:, jb * LANES:(jb + 1) * LANES])
        return out

    def row_max(lane_wise):
        return jnp.broadcast_to(jnp.max(lane_wise, axis=-1, keepdims=True), lane_wise.shape)

    def probs(sc, mb):
        ps = []
        ls = None
        for jb in range(sc.shape[1] // LANES):
            pj = jnp.exp2(sc[:, jb * LANES:(jb + 1) * LANES] - mb)
            ls = pj if ls is None else ls + pj
            ps.append(pj.astype(BF16))
        return jnp.concatenate(ps, axis=-1), ls

    def finish(kv, ls, acc):
        out = acc * (1.0 / jnp.sum(ls, axis=-1, keepdims=True))
        pair_out = [None, None]
        for gi in range(B_GROUP):
            h = kv * B_GROUP + gi
            oh = out[gi * tq:(gi + 1) * tq, :]
            if h % 2 != kv:
                oh = pltpu.roll(oh, HEAD_DIM, 1)
            pair_out[h % 2] = oh
            if h % 2 == 1:
                o_ref[0, :, (h // 2) * LANES:(h // 2 + 1) * LANES] = jnp.where(
                    upper, pair_out[1], pair_out[0]).astype(BF16)

    def context_only(kv):
        sc = scores(ctx_rows)
        p, ls = probs(sc, row_max(lane_max(sc)))
        finish(kv, ls, _dot(p, v_ref[0, ctx_rows, :]))

    def exact_row_max():
        def body(c, mx):
            rows = pl.ds(pl.multiple_of(n_ctx + c * tk, LANES), tk)
            return jnp.maximum(mx, lane_max(scores(rows)))

        return row_max(lax.fori_loop(0, n_lat, body, lane_max(scores(ctx_rows))))

    def all_keys(kv, shift_fn):
        mb = shift_fn()
        sc_next = scores(ctx_rows)
        ls = None
        acc = None
        for c in range(-1, n_lat):
            rows = ctx_rows if c < 0 else lat_rows(c)
            sc = sc_next
            if c + 1 < n_lat:
                sc_next = scores(lat_rows(c + 1))
            p, lt = probs(sc, mb)
            pv = _dot(p, v_ref[0, rows, :])
            ls = lt if ls is None else ls + lt
            acc = pv if acc is None else acc + pv
        finish(kv, ls, acc)

    for kv in range(B_KV_HEADS):
        for gi in range(B_GROUP):
            h = kv * B_GROUP + gi
            blk = q_ref[0, :, (h // 2) * LANES:(h // 2 + 1) * LANES].astype(F32)
            if h % 2 != kv:
                blk = pltpu.roll(blk, HEAD_DIM, 1)
            keep = upper if kv == 1 else jnp.logical_not(upper)
            qs_ref[gi * tq:(gi + 1) * tq, :] = jnp.where(keep, blk, 0.0).astype(BF16)
        qsq = qs_ref[...].astype(F32)
        qn2 = jnp.sum(qsq * qsq, axis=-1, keepdims=True)
        kn2 = jnp.max(kmax_ref[:, kv * HEAD_DIM:(kv + 1) * HEAD_DIM], axis=-1, keepdims=True)
        bound = jnp.broadcast_to(jnp.sqrt(qn2 * kn2) * BOUND_SLACK, (B_GROUP * tq, LANES))
        bound_ok = jnp.max(bound) <= MAX_SAFE_SHIFT
        pl.when(jnp.logical_and(is_latent, bound_ok))(
            functools.partial(all_keys, kv, lambda: bound))
        pl.when(jnp.logical_and(is_latent, jnp.logical_not(bound_ok)))(
            functools.partial(all_keys, kv, exact_row_max))
        pl.when(jnp.logical_not(is_latent))(functools.partial(context_only, kv))


def _gqa(bq, bk, bv, n_ctx):
    b, s, w = bq.shape
    tq = Q_TILE
    return pl.pallas_call(
        functools.partial(_gqa_kernel, n_ctx=n_ctx, s=s),
        grid=(b, s // tq),
        in_specs=[
            pl.BlockSpec((1, tq, w), lambda i, j: (i, j, 0)),
            pl.BlockSpec((1, s, LANES), lambda i, j: (i, 0, 0)),
            pl.BlockSpec((1, s, LANES), lambda i, j: (i, 0, 0)),
        ],
        out_specs=pl.BlockSpec((1, tq, w), lambda i, j: (i, j, 0)),
        out_shape=jax.ShapeDtypeStruct((b, s, w), BF16),
        scratch_shapes=[pltpu.VMEM((B_GROUP * tq, LANES), BF16), pltpu.VMEM((1, LANES), F32)],
        compiler_params=_params("parallel", "arbitrary"),
        name="gqa_attention",
    )(bq, bk, bv)


def _natten_bias(rel_bias):
    h = rel_bias.shape[0]
    qcol = np.arange(GRID_W)
    c0 = np.clip(qcol - WIN_C // 2, 0, GRID_W - WIN_C)
    kcol = np.arange(GRID_W)
    valid = (kcol[None, :] >= c0[:, None]) & (kcol[None, :] < c0[:, None] + WIN_C)
    dc = kcol[None, :] - qcol[:, None] + (WIN_C - 1)
    pick = (dc[None, :, :] == np.arange(2 * WIN_C - 1)[:, None, None]) & valid[None]
    cols = jnp.einsum('hdi,iqk->hdqk', rel_bias.astype(F32), jnp.asarray(pick, F32),
                      precision=lax.Precision.HIGHEST)
    cols = jnp.where(valid[None, None], cols, NEG_BIG)
    per_class = [jnp.swapaxes(cols[:, WIN_R - 1 - c:2 * WIN_R - 1 - c], 1, 2) for c in range(WIN_R)]
    return jnp.stack(per_class, axis=1).reshape(h, WIN_R, GRID_W, WIN_R * GRID_W)


def _natten_kernel(q_ref, k_ref, v_ref, bias_ref, o_ref, *, n_ctx, grid_rows):
    w = GRID_W
    band = WIN_R * w
    lane = lax.broadcasted_iota(jnp.int32, (w, LANES), 1)
    upper = lane >= HEAD_DIM
    kc = k_ref[0, 0:n_ctx, :]
    vc = v_ref[0, 0:n_ctx, :]

    def heads(q, fn):
        outs = []
        for hh in range(2):
            keep = upper if hh == 1 else jnp.logical_not(upper)
            outs.append(fn(hh, jnp.where(keep, q, jnp.zeros_like(q))))
        return jnp.where(upper, outs[1], outs[0])

    def ctx_block(i, carry):
        rows = pl.ds(pl.multiple_of(i * w, w), w)

        def attend(hh, qh):
            sc = _dot_nt(qh, kc)
            p = jnp.exp(sc - jnp.max(sc, axis=-1, keepdims=True))
            return _dot(p.astype(BF16), vc) / jnp.sum(p, axis=-1, keepdims=True)

        o_ref[0, rows, :] = heads(q_ref[0, rows, :], attend).astype(BF16)
        return carry

    lax.fori_loop(0, n_ctx // w, ctx_block, 0)

    nr = NAT_ROWS

    def row_group(gi, carry):
        qrows = pl.ds(pl.multiple_of(n_ctx + gi * (nr * w), nr * w), nr * w)
        qg = q_ref[0, qrows, :]
        lane_g = lax.broadcasted_iota(jnp.int32, qg.shape, 1) >= HEAD_DIM
        qh = (jnp.where(lane_g, jnp.zeros_like(qg), qg), jnp.where(lane_g, qg, jnp.zeros_like(qg)))
        s_ctx = [_dot_nt(qh[hh], kc) for hh in range(2)]
        s_win = {}
        vbs = []
        for rr in range(nr):
            r = gi * nr + rr
            r0 = jnp.clip(r - WIN_R // 2, 0, grid_rows - WIN_R)
            cls = r - r0
            krows = pl.ds(pl.multiple_of(n_ctx + r0 * w, w), band)
            kb = k_ref[0, krows, :]
            vbs.append(v_ref[0, krows, :])
            for hh in range(2):
                s_win[rr, hh] = _dot_nt(qh[hh][rr * w:(rr + 1) * w], kb) + bias_ref[hh, cls]
        p_win = {}
        p_ctx = [[], []]
        den = {}
        for rr in range(nr):
            for hh in range(2):
                sw = s_win[rr, hh]
                sc = s_ctx[hh][rr * w:(rr + 1) * w]
                m = jnp.maximum(jnp.max(sw, axis=-1, keepdims=True),
                                jnp.max(sc, axis=-1, keepdims=True))
                pw = jnp.exp(sw - m)
                pc = jnp.exp(sc - m)
                den[rr, hh] = jnp.sum(pw, axis=-1, keepdims=True) + jnp.sum(pc, axis=-1, keepdims=True)
                p_win[rr, hh] = pw.astype(BF16)
                p_ctx[hh].append(pc.astype(BF16))
        o_ctx = [_dot(jnp.concatenate(p_ctx[hh], axis=0), vc) for hh in range(2)]
        o_win = {(rr, hh): _dot(p_win[rr, hh], vbs[rr]) for rr in range(nr) for hh in range(2)}
        for rr in range(nr):
            outs = [(o_win[rr, hh] + o_ctx[hh][rr * w:(rr + 1) * w]) / den[rr, hh] for hh in range(2)]
            rows = pl.ds(pl.multiple_of(n_ctx + (gi * nr + rr) * w, w), w)
            o_ref[0, rows, :] = jnp.where(upper, outs[1], outs[0]).astype(BF16)
        return carry

    lax.fori_loop(0, grid_rows // nr, row_group, 0)


def _natten(cq, ck, cv, bias, layer, n_ctx):
    b, s, w = cq.shape
    pairs = w // LANES
    grid_rows = (s - n_ctx) // GRID_W
    assert grid_rows >= WIN_R and grid_rows % NAT_ROWS == 0
    blk = lambda: pl.BlockSpec((1, s, LANES), lambda i, p: (i, 0, p))
    return pl.pallas_call(
        functools.partial(_natten_kernel, n_ctx=n_ctx, grid_rows=grid_rows),
        grid=(b, w // LANES),
        in_specs=[blk(), blk(), blk(),
                  pl.BlockSpec((2, WIN_R, GRID_W, WIN_R * GRID_W),
                               lambda i, p: (layer * pairs + p, 0, 0, 0))],
        out_specs=blk(),
        out_shape=jax.ShapeDtypeStruct((b, s, w), BF16),
        compiler_params=_params("parallel", "parallel"),
        name="neighborhood_attention",
    )(cq, ck, cv, bias)


def _merge_ffn_kernel(xc_ref, xl_ref, oaf_ref, oab_ref, og_ref, yb_ref, yc_ref, gt_ref, mod_ref, gn_ref, wbr_ref,
                      wo_ref, nw_ref, wgu_ref, wd_ref, o_ref, *, d, hidden, n_split, tile0,
                      n_ctx_tiles):
    ya = []
    for h in range(A_HEADS):
        sl = slice(h * A_DK, (h + 1) * A_DK)
        oh = oaf_ref[0, :, sl] + oab_ref[0, :, sl]
        yh = oh * lax.rsqrt(jnp.mean(oh * oh, axis=-1, keepdims=True) + EPS) * gn_ref[...]
        ya.append((yh * _silu(og_ref[0, :, sl].astype(F32))).astype(BF16))
    ys = (jnp.concatenate(ya, axis=-1), yb_ref[0], yc_ref[0])
    merged = None
    for k in range(N_BRANCH):
        term = gt_ref[0, :, k * d:(k + 1) * d].astype(F32) * _dot(ys[k], wbr_ref[0, k])
        merged = term if merged is None else merged + term
    g1 = mod_ref[0, 0, :, 2 * d:3 * d]
    x = _stream_tile(xc_ref, xl_ref, pl.program_id(1) + tile0, n_ctx_tiles)
    x = x + g1 * _dot(merged.astype(BF16), wo_ref[0])

    y = x * lax.rsqrt(jnp.mean(x * x, axis=-1, keepdims=True) + EPS) * nw_ref[...]
    sh2 = mod_ref[0, 0, :, 3 * d:4 * d]
    sc2 = mod_ref[0, 0, :, 4 * d:5 * d]
    g2 = mod_ref[0, 0, :, 5 * d:6 * d]
    h = (y * (1.0 + sc2) + sh2).astype(BF16)
    step = hidden // n_split
    acc = None
    for c in range(n_split):
        a = _dot(h, wgu_ref[0, :, c * step:(c + 1) * step])
        g = _dot(h, wgu_ref[0, :, hidden + c * step:hidden + (c + 1) * step])
        part = _dot((_silu(a) * g).astype(BF16), wd_ref[0, c * step:(c + 1) * step, :])
        acc = part if acc is None else acc + part
    o_ref[0] = x + g2 * acc


def _merge_ffn(x_ctx, x_lat, lat_base, oaf, oab, og, yb, yc, gt, modsel, gn, wbr, wo, norm_w, wgu, wd,
               layer, n_ctx_tiles, tile0, n_tiles):
    b, _, d = x_ctx.shape
    tm = TOKEN_TILE
    hidden = wd.shape[1]
    n_split = 1
    assert hidden % (n_split * LANES) == 0
    tok = lambda w: pl.BlockSpec((1, tm, w), lambda i, j: (i, j + tile0, 0))
    return pl.pallas_call(
        functools.partial(_merge_ffn_kernel, d=d, hidden=hidden, n_split=n_split, tile0=tile0,
                          n_ctx_tiles=n_ctx_tiles),
        grid=(b, n_tiles),
        in_specs=[
            *_stream_specs(tm, d, n_ctx_tiles, lat_base, tile0), tok(oaf.shape[-1]), tok(oab.shape[-1]), tok(og.shape[-1]), tok(yb.shape[-1]), tok(yc.shape[-1]),
            tok(gt.shape[-1]),
            pl.BlockSpec((1, 1, 1, modsel.shape[-1]),
                         lambda i, j: (i, (j + tile0 >= n_ctx_tiles).astype(jnp.int32), 0, 0)),
            _resident((1, A_DK), lambda i, j: (0, 0)),
            _resident((1,) + wbr.shape[1:], lambda i, j: (layer, 0, 0, 0)),
            _resident((1,) + wo.shape[1:], lambda i, j: (layer, 0, 0)),
            _resident((1, d), lambda i, j: (0, 0)),
            _resident((1,) + wgu.shape[1:], lambda i, j: (layer, 0, 0)),
            _resident((1,) + wd.shape[1:], lambda i, j: (layer, 0, 0)),
        ],
        out_specs=pl.BlockSpec((1, tm, d), lambda i, j: (i, j, 0)),
        out_shape=jax.ShapeDtypeStruct((b, n_tiles * tm, d), F32),
        compiler_params=_params("parallel", "parallel"),
        name="merge_ffn",
    )(x_ctx, x_lat, oaf, oab, og, yb, yc, gt, modsel, gn, wbr, wo, norm_w, wgu, wd)


def _rope_tables(t, n_ctx):
    pos = jnp.arange(t)
    row = (pos // GRID_W).astype(F32)
    colp = (pos % GRID_W).astype(F32)
    n = HEAD_DIM // 4
    inv = ROPE_THETA ** (-jnp.arange(n, dtype=F32) / n)
    ang = jnp.concatenate([row[:, None] * inv, colp[:, None] * inv], axis=-1)
    cos = jnp.cos(ang)
    sin = jnp.sin(ang)
    reps = LANES // HEAD_DIM
    cos_t = jnp.tile(jnp.concatenate([cos, cos], axis=-1), (1, reps))
    sin_t = jnp.tile(jnp.concatenate([-sin, sin], axis=-1), (1, reps))
    cos_t = jnp.concatenate([jnp.ones((n_ctx, LANES), F32), cos_t], axis=0)
    sin_t = jnp.concatenate([jnp.zeros((n_ctx, LANES), F32), sin_t], axis=0)
    return cos_t, sin_t


def _deinterleave_heads(a):
    lead = a.shape[:-1]
    n = a.shape[-1] // HEAD_DIM
    a = a.reshape(lead + (n, HEAD_DIM // 2, 2))
    return jnp.swapaxes(a, -1, -2).reshape(lead + (n * HEAD_DIM,))


def _inproj_weight(w):
    b0 = 5 * A_HEADS * A_DK
    b1 = b0 + (B_HEADS + B_KV_HEADS) * HEAD_DIM
    return w.astype(BF16).at[..., b0:b1].set(_deinterleave_heads(w[..., b0:b1]).astype(BF16))


def kernel(x, c, ctx, c_ctx, w_mod, b_mod, norm_mix, norm_ffn, w_in, lb_raw, gn_a, qn_b, kn_b,
           qn_c, kn_c, rel_bias, w_branch, w_out, w_gate_up, w_down):
    bn, t, d = x.shape
    n_ctx = ctx.shape[1]
    depth = w_mod.shape[0]
    tm = TOKEN_TILE
    assert n_ctx % tm == 0 and t % tm == 0 and t % GRID_W == 0
    n_ctx_tiles = n_ctx // tm
    aw = A_HEADS * A_DK

    c_all = jnp.zeros((8, d), F32).at[:bn].set(c).at[bn].set(c_ctx)
    mod = _modulation(c_all, w_mod, b_mod)
    modsel = jnp.stack([jnp.broadcast_to(mod[:, bn:bn + 1], (depth, bn, 6 * d)), mod[:, :bn]],
                       axis=2)[:, :, :, None, :]

    lbp = jax.nn.softmax(lb_raw.astype(F32), axis=0)
    lb_all = jnp.clip(jnp.cumsum(lbp, axis=0) - lbp[:1], 0.0, 1.0 - 1e-6)
    zeros2 = jnp.zeros_like(lb_all)
    lbc_all = jnp.concatenate([jnp.log(lb_all + LB_TINY), jnp.log1p(-lb_all), 1.0 - lb_all, zeros2],
                              axis=1)

    cos_t, sin_t = _rope_tables(t, n_ctx)
    lane = np.arange(LANES)
    gmat = (lane[:, None] // HEAD_DIM == lane[None, :] // HEAD_DIM) / HEAD_DIM
    gmat = jnp.asarray(np.concatenate([gmat, gmat], axis=0), BF16)
    dmat, smask, qrole = _hgrn_tables()
    tables = (jnp.asarray(np.concatenate([dmat, dmat], axis=2), BF16), jnp.asarray(smask),
              jnp.asarray(qrole))

    x_ctx, x_lat, lat_base = ctx, x, 0
    w_in_bf = _inproj_weight(w_in)
    w_br_bf, w_o_bf = w_branch.astype(BF16), w_out.astype(BF16)
    w_gu_bf, w_d_bf = w_gate_up.astype(BF16), w_down.astype(BF16)
    headw_all = jnp.stack([jnp.tile(_deinterleave_heads(qn_b), (1, B_HEADS)),
                           jnp.tile(_deinterleave_heads(kn_b), (1, B_HEADS)),
                           jnp.tile(qn_c, (1, C_HEADS)), jnp.tile(kn_c, (1, C_HEADS))],
                          axis=1).astype(F32)
    headw_all = jnp.concatenate([headw_all, jnp.zeros_like(headw_all)], axis=1)
    n_heads_c = rel_bias.shape[1]
    bias_all = _natten_bias(rel_bias.reshape((depth * n_heads_c,) + rel_bias.shape[2:]))
    for l in range(depth):
        last = l == depth - 1
        (aq, alff, akf, alfb, akb, av, aog, bq, bk, bv, cq, ck, cv, gt) = _in_projection(
            x_ctx, x_lat, lat_base, t + n_ctx, modsel[l], norm_mix[l][None, :], w_in_bf, l,
            lbc_all[l], headw_all[l], cos_t, sin_t, gmat, n_ctx_tiles)
        oaf, oab = _hgrn(aq, alff, akf, alfb, akb, av, tables, n_ctx)
        yb = _gqa(bq, bk, bv, n_ctx)
        yc = _natten(cq, ck, cv, bias_all, l, n_ctx)
        tile0 = n_ctx_tiles if last else 0
        n_tiles = (t if last else t + n_ctx) // tm
        xs = _merge_ffn(x_ctx, x_lat, lat_base, oaf, oab, aog, yb, yc, gt, modsel[l],
                        gn_a[l][None, :], w_br_bf, w_o_bf, norm_ffn[l][None, :], w_gu_bf, w_d_bf,
                        l, n_ctx_tiles, tile0, n_tiles)
        x_ctx, x_lat, lat_base = xs, xs, n_ctx_tiles
    return xs
```
